```python
import math
import jax, jax.numpy as jnp
from jax import lax
import numpy as np

D_MODEL = 1024
BATCH = 4
SEQ = 4096
DEPTH = 4
DEC_BATCH = 8
DEC_SEQ = 16
PAST_LEN = 4096

CHUNK = 64
Q_BLOCK = 128
N_EVEN = (DEPTH + 1) // 2
N_ODD = DEPTH // 2
D_A = D_MODEL
D_B = D_MODEL
CONV_W = 31
HB_DK = 128
HB_HEADS = D_B // HB_DK
HB_DV = D_B // HB_HEADS
D_C = 2 * D_MODEL
HC_HEADS = 16
HC_DIM = D_C // HC_HEADS // 2
IN_EVEN = 3 * D_A + 4 * D_B
IN_ODD = 4 * D_C
ALPHA = (2.0 * DEPTH) ** 0.25
BETA = (8.0 * DEPTH) ** -0.25
EPS = 1e-5

kernel_name = "hybrid_stream_conv_hgrn2_diffattn_step"


def layer_norm(x, g, b):
    xf = x.astype(jnp.float32)
    mu = jnp.mean(xf, axis=-1, keepdims=True)
    var = jnp.mean(jnp.square(xf - mu), axis=-1, keepdims=True)
    return ((xf - mu) * lax.rsqrt(var + EPS) * g.astype(jnp.float32) + b.astype(jnp.float32)).astype(x.dtype)


def rms_norm(x, g):
    xf = x.astype(jnp.float32)
    y = xf * lax.rsqrt(jnp.mean(jnp.square(xf), axis=-1, keepdims=True) + EPS)
    return (y * g.astype(jnp.float32)).astype(x.dtype)


def causal_depthwise_conv(u, buf, w, b):
    full = jnp.concatenate([buf.astype(u.dtype), u], axis=1)
    y = lax.conv_general_dilated(full, w[:, None, :].astype(u.dtype), window_strides=(1,), padding='VALID',
                                 dimension_numbers=('NWC', 'WIO', 'NWC'), feature_group_count=u.shape[-1])
    return y + b.astype(u.dtype), full[:, -(CONV_W - 1):]


def hgrn2_recurrence(q, k, v, logf, S0):
    Bsz, T, H, K = q.shape
    V = v.shape[-1]
    chunk = CHUNK if T % CHUNK == 0 else T
    n = T // chunk

    def blocks(a):
        return jnp.moveaxis(a.reshape(Bsz, n, chunk, *a.shape[2:]), 1, 0)

    tri = jnp.tril(jnp.ones((chunk, chunk), dtype=bool))[None, :, :, None, None]

    def step(S, inp):
        qc, kc, vc, gc = inp
        b = jnp.cumsum(gc, axis=1)
        o_inter = jnp.einsum('blhk,bhkv->blhv', qc * jnp.exp(b), S)
        diff = b[:, :, None] - b[:, None, :]
        decay = jnp.exp(jnp.where(tri, diff, -jnp.inf))
        att = jnp.einsum('bthk,btshk,bshk->btsh', qc, decay, kc)
        o = o_inter + jnp.einsum('btsh,bshv->bthv', att, vc)
        b_last = b[:, -1]
        S_new = jnp.exp(b_last)[..., None] * S + jnp.einsum(
            'bshk,bshv->bhkv', kc * jnp.exp(b_last[:, None] - b), vc)
        return S_new, o

    S, o = lax.scan(step, S0.astype(jnp.float32), (blocks(q), blocks(k), blocks(v), blocks(logf)))
    o = jnp.moveaxis(o, 0, 1).reshape(Bsz, T, H, V)
    return o, S


def even_mixer(x, conv_buf, S0, w_in, w_out, conv_w, conv_b, cln_g, cln_b, lb, hn_g):
    Bsz, T, _ = x.shape
    h = x @ w_in
    cuts = [D_A, 2 * D_A, 3 * D_A, 3 * D_A + D_B, 3 * D_A + 2 * D_B, 3 * D_A + 3 * D_B]
    a, ga, za, q, fr, i, zb = jnp.split(h, cuts, axis=-1)
    u = a * jax.nn.sigmoid(ga)
    c, new_buf = causal_depthwise_conv(u, conv_buf, conv_w, conv_b)
    c = jax.nn.silu(layer_norm(c, cln_g, cln_b))
    out_a = c * jax.nn.silu(za)
    hs = (Bsz, T, HB_HEADS, HB_DK)
    frf = fr.astype(jnp.float32).reshape(hs)
    lbh = lb.astype(jnp.float32).reshape(HB_HEADS, HB_DK)
    logf = jnp.logaddexp(jnp.log(lbh), jnp.log1p(-lbh) + jax.nn.log_sigmoid(frf))
    kk = (1.0 - lbh) * jax.nn.sigmoid(-frf)
    qq = jax.nn.silu(q.astype(jnp.float32)).reshape(hs)
    vv = i.astype(jnp.float32).reshape(Bsz, T, HB_HEADS, HB_DV)
    o, S = hgrn2_recurrence(qq, kk, vv, logf, S0)
    o = rms_norm(o, hn_g).astype(x.dtype).reshape(Bsz, T, D_B)
    out_b = o * jax.nn.silu(zb)
    y = jnp.concatenate([out_a, out_b], axis=-1) @ w_out
    return y, new_buf, S


def diff_attend(q, k, v, q_pos, k_pos, lam):
    s = jnp.einsum('bqhcd,bkhcd->bhcqk', q, k).astype(jnp.float32) * (HC_DIM ** -0.5)
    mask = (k_pos[None, :] // CHUNK) <= (q_pos[:, None] // CHUNK)
    p = jax.nn.softmax(jnp.where(mask, s, -jnp.inf), axis=-1)
    p = p[:, :, 0] - lam * p[:, :, 1]
    return jnp.einsum('bhqk,bkhe->bqhe', p.astype(v.dtype), v)


def odd_mixer(x, k_past, v_past, w_in, w_out, lq1, lk1, lq2, lk2, sub_g, lam_init):
    Bsz, T, _ = x.shape
    q, k, v, z = jnp.split(x @ w_in, 4, axis=-1)
    q = q.reshape(Bsz, T, HC_HEADS, 2, HC_DIM)
    k = k.reshape(Bsz, T, HC_HEADS, 2, HC_DIM)
    v = v.reshape(Bsz, T, HC_HEADS, 2 * HC_DIM)
    f32 = jnp.float32
    lam = (jnp.exp(jnp.sum(lq1.astype(f32) * lk1.astype(f32))) - jnp.exp(jnp.sum(lq2.astype(f32) * lk2.astype(f32)))
           + lam_init)
    if k_past is None:
        nb = T // Q_BLOCK
        qb = jnp.moveaxis(q.reshape(Bsz, nb, Q_BLOCK, HC_HEADS, 2, HC_DIM), 1, 0)
        k_pos = jnp.arange(T)

        def one(args):
            qi, start = args
            return diff_attend(qi, k, v, start + jnp.arange(Q_BLOCK), k_pos, lam)

        o = lax.map(one, (qb, jnp.arange(nb) * Q_BLOCK))
        o = jnp.moveaxis(o, 0, 1).reshape(Bsz, T, HC_HEADS, 2 * HC_DIM)
    else:
        P = k_past.shape[1]
        k_all = jnp.concatenate([k_past.astype(k.dtype).reshape(Bsz, P, HC_HEADS, 2, HC_DIM), k], axis=1)
        v_all = jnp.concatenate([v_past.astype(v.dtype), v], axis=1)
        o = diff_attend(q, k_all, v_all, P + jnp.arange(T), jnp.arange(P + T), lam)
    o = rms_norm(o, sub_g) * (1.0 - lam_init)
    o = o.reshape(Bsz, T, D_C) * jax.nn.silu(z)
    return o @ w_out, k.reshape(Bsz, T, HC_HEADS, 2 * HC_DIM), v


def setup_inputs(seed: int = 0) -> dict:
    key = jax.random.key(seed)
    ks = jax.random.split(key, 24)
    nrm = jax.random.normal
    f32 = jnp.float32
    return {
        "x_prompt": nrm(ks[0], (BATCH, SEQ, D_MODEL), f32),
        "x_sample": nrm(ks[1], (DEC_BATCH, DEC_SEQ, D_MODEL), f32),
        "state_conv": 0.5 * nrm(ks[2], (N_EVEN, DEC_BATCH, CONV_W - 1, D_A), f32),
        "state_hgrn": 0.5 * nrm(ks[3], (N_EVEN, DEC_BATCH, HB_HEADS, HB_DK, HB_DV), f32),
        "cache_k": nrm(ks[4], (N_ODD, DEC_BATCH, PAST_LEN, HC_HEADS, 2 * HC_DIM), f32),
        "cache_v": nrm(ks[5], (N_ODD, DEC_BATCH, PAST_LEN, HC_HEADS, 2 * HC_DIM), f32),
        "w_in_even": nrm(ks[6], (N_EVEN, D_MODEL, IN_EVEN), f32) * D_MODEL ** -0.5,
        "w_out_even": nrm(ks[7], (N_EVEN, D_A + D_B, D_MODEL), f32) * (D_A + D_B) ** -0.5 * BETA,
        "conv_w": nrm(ks[8], (N_EVEN, CONV_W, D_A), f32) * CONV_W ** -0.5,
        "conv_b": 0.01 * nrm(ks[9], (N_EVEN, D_A), f32),
        "conv_ln_g": 1.0 + 0.02 * nrm(ks[10], (N_EVEN, D_A), f32),
        "conv_ln_b": 0.02 * nrm(ks[11], (N_EVEN, D_A), f32),
        "hgrn_lb_logits": nrm(ks[12], (N_EVEN, D_B), f32),
        "hgrn_norm_g": 1.0 + 0.02 * nrm(ks[13], (N_EVEN, HB_DV), f32),
        "w_in_odd": nrm(ks[14], (N_ODD, D_MODEL, IN_ODD), f32) * D_MODEL ** -0.5,
        "w_out_odd": nrm(ks[15], (N_ODD, D_C, D_MODEL), f32) * D_C ** -0.5 * BETA,
        "lam_q1": 0.1 * nrm(ks[16], (N_ODD, HC_DIM), f32),
        "lam_k1": 0.1 * nrm(ks[17], (N_ODD, HC_DIM), f32),
        "lam_q2": 0.1 * nrm(ks[18], (N_ODD, HC_DIM), f32),
        "lam_k2": 0.1 * nrm(ks[19], (N_ODD, HC_DIM), f32),
        "subln_g": 1.0 + 0.02 * nrm(ks[20], (N_ODD, 2 * HC_DIM), f32),
        "post_ln_g": 1.0 + 0.02 * nrm(ks[21], (DEPTH, D_MODEL), f32),
        "post_ln_b": 0.02 * nrm(ks[22], (DEPTH, D_MODEL), f32),
    }


def reference(x_prompt, x_sample, state_conv, state_hgrn, cache_k, cache_v, w_in_even, w_out_even, conv_w, conv_b,
              conv_ln_g, conv_ln_b, hgrn_lb_logits, hgrn_norm_g, w_in_odd, w_out_odd, lam_q1, lam_k1, lam_q2, lam_k2,
              subln_g, post_ln_g, post_ln_b):
    lb_cum = jnp.cumsum(jax.nn.softmax(hgrn_lb_logits.astype(jnp.float32), axis=0), axis=0)
    lb_all = lb_cum - lb_cum[0:1]
    xp, xs = x_prompt, x_sample
    bp = xp.shape[0]
    conv_p, hgrn_p, k_p, v_p = [], [], [], []
    conv_s, hgrn_s, k_s, v_s = [], [], [], []
    for l in range(DEPTH):
        if l % 2 == 0:
            e = l // 2
            prm = (w_in_even[e], w_out_even[e], conv_w[e], conv_b[e], conv_ln_g[e], conv_ln_b[e], lb_all[e],
                   hgrn_norm_g[e])
            buf0 = jnp.zeros((bp, CONV_W - 1, D_A), xp.dtype)
            S0 = jnp.zeros((bp, HB_HEADS, HB_DK, HB_DV), jnp.float32)
            yp, cb_p, S_p = even_mixer(xp, buf0, S0, *prm)
            ys, cb_s, S_s = even_mixer(xs, state_conv[e], state_hgrn[e], *prm)
            conv_p.append(cb_p); hgrn_p.append(S_p)
            conv_s.append(cb_s); hgrn_s.append(S_s)
        else:
            o = l // 2
            lam_init = 0.8 - 0.6 * math.exp(-0.3 * l)
            prm = (w_in_odd[o], w_out_odd[o], lam_q1[o], lam_k1[o], lam_q2[o], lam_k2[o], subln_g[o], lam_init)
            yp, kn_p, vn_p = odd_mixer(xp, None, None, *prm)
            ys, kn_s, vn_s = odd_mixer(xs, cache_k[o], cache_v[o], *prm)
            k_p.append(kn_p); v_p.append(vn_p)
            k_s.append(kn_s); v_s.append(vn_s)
        xp = layer_norm(ALPHA * xp + yp, post_ln_g[l], post_ln_b[l])
        xs = layer_norm(ALPHA * xs + ys, post_ln_g[l], post_ln_b[l])
    return (xp, xs, jnp.stack(conv_p), jnp.stack(hgrn_p), jnp.stack(k_p), jnp.stack(v_p),
            jnp.stack(conv_s), jnp.stack(hgrn_s), jnp.stack(k_s), jnp.stack(v_s))
```

```python
import functools
import math

import jax
import jax.numpy as jnp
from jax import lax
from jax.experimental import pallas as pl
from jax.experimental.pallas import tpu as pltpu

F32 = jnp.float32
BF16 = jnp.bfloat16

CHUNK = 64
SUB = 16
CONV_W = 31
HIST = 32
HB_DK = 128
HC_DIM = 64
LANES = 128
EPS = 1e-5
VMEM_LIMIT = 48 * 1024 * 1024


def _params(sem):
    return pltpu.CompilerParams(dimension_semantics=sem, vmem_limit_bytes=VMEM_LIMIT)


def _dot(a, b):
    return jnp.dot(a, b, preferred_element_type=F32)


def _dot_nt(a, b):
    return lax.dot_general(a, b, (((1,), (1,)), ((), ())), preferred_element_type=F32)


def _dot_tn(a, b):
    return lax.dot_general(a, b, (((0,), (0,)), ((), ())), preferred_element_type=F32)


def _sigmoid(x):
    return 1.0 / (1.0 + jnp.exp(-x))


def _silu(x):
    return x * _sigmoid(x)


def _linear_kernel(x_ref, w_ref, o_ref):
    o_ref[...] = _dot(x_ref[...].astype(BF16), w_ref[...]).astype(o_ref.dtype)


def _linear(x, w, col0, ncols, out_dtype, tm, tn):
    n, k = x.shape
    assert n % tm == 0 and ncols % tn == 0 and col0 % tn == 0
    jb = col0 // tn
    return pl.pallas_call(
        _linear_kernel,
        grid=(n // tm, ncols // tn),
        in_specs=[pl.BlockSpec((tm, k), lambda i, j: (i, 0)),
                  pl.BlockSpec((k, tn), lambda i, j: (0, jb + j))],
        out_specs=pl.BlockSpec((tm, tn), lambda i, j: (i, j)),
        out_shape=jax.ShapeDtypeStruct((n, ncols), out_dtype),
        compiler_params=_params(("parallel", "arbitrary")),
        name="linear",
    )(x, w)


def _conv_kernel(a_ref, ga_ref, za_ref, buf_ref, w_ref, cb_ref, g_ref, b_ref, out_ref, nb_ref, scr, c_scr,
                 *, tt, rb):
    t = pl.program_id(1)
    nch = a_ref.shape[-1] // LANES

    @pl.when(t == 0)
    def _():
        scr[0:HIST, :] = buf_ref[...]

    @pl.when(t > 0)
    def _():
        scr[0:HIST, :] = scr[tt:tt + HIST, :]

    scr[HIST:HIST + tt, :] = a_ref[...] * _sigmoid(ga_ref[...])

    def row_block(r, carry):
        r0 = pl.multiple_of(r * rb, rb)
        for c in range(nch):
            cs = slice(c * LANES, (c + 1) * LANES)
            slab = scr[pl.ds(r0, rb + HIST), cs]
            acc = jnp.zeros((rb, LANES), F32)
            for j in range(CONV_W):
                acc = acc + w_ref[j:j + 1, cs] * slab[j + 2:j + 2 + rb, :]
            c_scr[pl.ds(r0, rb), cs] = acc + cb_ref[:, cs]
        return carry

    lax.fori_loop(0, tt // rb, row_block, 0)

    cv = c_scr[...]
    mu = jnp.mean(cv, axis=-1, keepdims=True)
    xc = cv - mu
    var = jnp.mean(xc * xc, axis=-1, keepdims=True)
    y = xc * lax.rsqrt(var + EPS) * g_ref[...] + b_ref[...]
    out_ref[...] = _silu(y) * _silu(za_ref[...])

    @pl.when(t == pl.num_programs(1) - 1)
    def _():
        nb_ref[...] = scr[tt:tt + HIST, :]


def _conv_branch(h3, buf, cw, cb, g, b, tt):
    bsz, tlen, _ = h3.shape
    c = cw.shape[-1]
    assert tlen % tt == 0
    rb = min(32, tt)
    assert tt % rb == 0
    row = lambda col: pl.BlockSpec((None, tt, c), lambda bi, ti: (bi, ti, col))
    vec = lambda r: pl.BlockSpec((r, c), lambda bi, ti: (0, 0))
    hist = pl.BlockSpec((None, HIST, c), lambda bi, ti: (bi, 0, 0))
    return pl.pallas_call(
        functools.partial(_conv_kernel, tt=tt, rb=rb),
        grid=(bsz, tlen // tt),
        in_specs=[row(0), row(1), row(2), hist, vec(CONV_W), vec(1), vec(1), vec(1)],
        out_specs=[pl.BlockSpec((None, tt, c), lambda bi, ti: (bi, ti, 0)), hist],
        out_shape=[jax.ShapeDtypeStruct((bsz, tlen, c), F32), jax.ShapeDtypeStruct((bsz, HIST, c), F32)],
        scratch_shapes=[pltpu.VMEM((tt + HIST, c), F32), pltpu.VMEM((tt, c), F32)],
        compiler_params=_params(("parallel", "arbitrary")),
        name="conv_branch",
    )(h3, h3, h3, buf, cw, cb, g, b)


def _cumsum_rows(tri, g):
    g0 = g.astype(BF16)
    r1 = g - g0.astype(F32)
    g1 = r1.astype(BF16)
    g2 = (r1 - g1.astype(F32)).astype(BF16)
    return _dot(tri, g0) + _dot(tri, g1) + _dot(tri, g2)


def _hgrn_kernel(q_ref, f_ref, i_ref, zb_ref, lbl_ref, hg_ref, s0_ref, o_ref, s_ref, st_scr,
                 *, layer, chunk, nchunks, heads):
    t = pl.program_id(2)
    dk = HB_DK

    @pl.when(t == 0)
    def _():
        for h in range(heads):
            st_scr[h] = s0_ref[h].T

    lg = lbl_ref[...]
    ex = jnp.exp(lg - jnp.max(lg, axis=0, keepdims=True))
    sm = ex / jnp.sum(ex, axis=0, keepdims=True)
    lb = jnp.zeros_like(sm[0:1])
    for r in range(1, layer + 1):
        lb = lb + sm[r:r + 1]
    log_lb = jnp.log(lb)
    log1m_lb = jnp.log1p(-lb)

    rows = lax.broadcasted_iota(jnp.int32, (chunk, chunk), 0)
    cols = lax.broadcasted_iota(jnp.int32, (chunk, chunk), 1)
    tri = jnp.where(rows >= cols, 1.0, 0.0).astype(BF16)
    sub_rows = lax.broadcasted_iota(jnp.int32, (SUB, 1), 0)
    nsub = chunk // SUB

    def one_chunk(ci, carry):
        r0 = pl.multiple_of(ci * chunk, chunk)
        rs = pl.ds(r0, chunk)
        for h in range(heads):
            cs = slice(h * dk, (h + 1) * dk)
            fr = f_ref[rs, cs]
            lsig = jnp.minimum(fr, 0.0) - jnp.log1p(jnp.exp(-jnp.abs(fr)))
            c2 = log1m_lb[:, cs] + lsig
            a2 = log_lb[:, cs]
            logf = jnp.maximum(a2, c2) + jnp.log1p(jnp.exp(-jnp.abs(a2 - c2)))
            kk = (1.0 - lb[:, cs]) * _sigmoid(-fr)
            qq = _silu(q_ref[rs, cs])
            vv = i_ref[rs, cs]
            b = _cumsum_rows(tri, logf)
            st = st_scr[h]
            o = _dot_nt((qq * jnp.exp(b)).astype(BF16), st.astype(BF16))
            parts = []
            for i in range(nsub):
                sl = slice(i * SUB, (i + 1) * SUB)
                b_i, q_i, k_i, v_i = b[sl], qq[sl], kk[sl], vv[sl]
                o_i = o[sl]
                if i > 0:
                    bref = b[i * SUB - 1:i * SUB]
                    qs = q_i * jnp.exp(b_i - bref)
                    ks = kk[0:i * SUB] * jnp.exp(bref - b[0:i * SUB])
                    att = _dot_nt(qs.astype(BF16), ks.astype(BF16))
                    o_i = o_i + _dot(att.astype(BF16), vv[0:i * SUB].astype(BF16))
                for s in range(SUB):
                    d = jnp.where(sub_rows >= s, b_i - b_i[s:s + 1], -jnp.inf)
                    p = q_i * (k_i[s:s + 1] * jnp.exp(d))
                    o_i = o_i + jnp.sum(p, axis=-1, keepdims=True) * v_i[s:s + 1]
                parts.append(o_i)
            o = jnp.concatenate(parts, axis=0) if nsub > 1 else parts[0]
            b_last = b[chunk - 1:chunk]
            kd = kk * jnp.exp(b_last - b)
            st_scr[h] = st * jnp.exp(b_last) + _dot_tn(vv.astype(BF16), kd.astype(BF16))
            on = o * lax.rsqrt(jnp.mean(o * o, axis=-1, keepdims=True) + EPS) * hg_ref[...]
            o_ref[rs, cs] = on * _silu(zb_ref[rs, cs])
        return carry

    lax.fori_loop(0, nchunks, one_chunk, 0)

    @pl.when(t == pl.num_programs(2) - 1)
    def _():
        for h in range(heads):
            s_ref[h] = st_scr[h].T


def _hgrn_branch(h3, lb_logits, hn_g, s0, layer, col_q, tc, heads):
    bsz, tlen, _ = h3.shape
    nh, dk, dv = s0.shape[1:]
    d_b = nh * dk
    chunk = CHUNK if tlen % CHUNK == 0 else tlen
    assert chunk % SUB == 0 and tc % chunk == 0 and tlen % tc == 0 and nh % heads == 0
    w = heads * dk
    gpb = d_b // w
    col = lambda k: pl.BlockSpec((None, tc, w), lambda bi, hi, ti: (bi, ti, (col_q + k) * gpb + hi))
    st = pl.BlockSpec((None, heads, dk, dv), lambda bi, hi, ti: (bi, hi, 0, 0))
    ne = lb_logits.shape[0]
    return pl.pallas_call(
        functools.partial(_hgrn_kernel, layer=layer, chunk=chunk, nchunks=tc // chunk, heads=heads),
        grid=(bsz, nh // heads, tlen // tc),
        in_specs=[col(0), col(1), col(2), col(3),
                  pl.BlockSpec((ne, w), lambda bi, hi, ti: (0, hi)),
                  pl.BlockSpec((1, dv), lambda bi, hi, ti: (0, 0)),
                  st],
        out_specs=[pl.BlockSpec((None, tc, w), lambda bi, hi, ti: (bi, ti, hi)), st],
        out_shape=[jax.ShapeDtypeStruct((bsz, tlen, d_b), F32), jax.ShapeDtypeStruct(s0.shape, F32)],
        scratch_shapes=[pltpu.VMEM((heads, dv, dk), F32)],
        compiler_params=_params(("parallel", "parallel", "arbitrary")),
        name="hgrn_branch",
    )(h3, h3, h3, h3, lb_logits, hn_g, s0)


def _out_ln_kernel(a_ref, b_ref, w_ref, x_ref, g_ref, bb_ref, o_ref, *, alpha):
    ka = a_ref.shape[-1]
    y = _dot(a_ref[...].astype(BF16), w_ref[0:ka, :]) + _dot(b_ref[...].astype(BF16), w_ref[ka:, :])
    r = alpha * x_ref[...] + y
    mu = jnp.mean(r, axis=-1, keepdims=True)
    rc = r - mu
    var = jnp.mean(rc * rc, axis=-1, keepdims=True)
    o_ref[...] = rc * lax.rsqrt(var + EPS) * g_ref[...] + bb_ref[...]


def _out_ln(a, a_col, b, b_col, w, x, g, bb, alpha, tm):
    n, d = x.shape
    kh = w.shape[0] // 2
    assert n % tm == 0
    return pl.pallas_call(
        functools.partial(_out_ln_kernel, alpha=alpha),
        grid=(n // tm,),
        in_specs=[pl.BlockSpec((tm, kh), lambda i: (i, a_col)),
                  pl.BlockSpec((tm, kh), lambda i: (i, b_col)),
                  pl.BlockSpec(w.shape, lambda i: (0, 0)),
                  pl.BlockSpec((tm, d), lambda i: (i, 0)),
                  pl.BlockSpec((1, d), lambda i: (0, 0)),
                  pl.BlockSpec((1, d), lambda i: (0, 0))],
        out_specs=pl.BlockSpec((tm, d), lambda i: (i, 0)),
        out_shape=jax.ShapeDtypeStruct((n, d), F32),
        compiler_params=_params(("parallel",)),
        name="out_ln",
    )(a, b, w, x, g, bb)


def _lambda(lq1, lk1, lq2, lk2, lam_init):
    e1 = jnp.exp(jnp.sum(lq1[...] * lk1[...], axis=-1, keepdims=True))
    e2 = jnp.exp(jnp.sum(lq2[...] * lk2[...], axis=-1, keepdims=True))
    return e1 - e2 + lam_init


def _map_queries(q_ref):
    q = q_ref[...] * (HC_DIM ** -0.5)
    lane = lax.broadcasted_iota(jnp.int32, q.shape, 1)
    return (jnp.where(lane < HC_DIM, q, 0.0).astype(BF16), jnp.where(lane >= HC_DIM, q, 0.0).astype(BF16))


def _attn_finish(o1, o2, lam, sg_ref, z_ref, o_ref, lam_init):
    o = o1 - lam * o2
    o = o * lax.rsqrt(jnp.mean(o * o, axis=-1, keepdims=True) + EPS) * sg_ref[...] * (1.0 - lam_init)
    o_ref[...] = o * _silu(z_ref[...])


def _attn_prompt_kernel(q_ref, k_ref, v_ref, z_ref, lq1, lk1, lq2, lk2, sg_ref, o_ref, m_scr, l_scr, acc_scr,
                        *, tq, lam_init):
    qi = pl.program_id(2)
    qm = _map_queries(q_ref)
    m_scr[...] = jnp.full(m_scr.shape, -jnp.inf, F32)
    l_scr[...] = jnp.zeros(l_scr.shape, F32)
    acc_scr[...] = jnp.zeros(acc_scr.shape, F32)
    rq = lax.broadcasted_iota(jnp.int32, (tq, tq), 0) // CHUNK
    ck = lax.broadcasted_iota(jnp.int32, (tq, tq), 1) // CHUNK
    diag_mask = ck <= rq

    def block(kj, masked):
        ks = pl.ds(pl.multiple_of(kj * tq, tq), tq)
        k = k_ref[ks, :].astype(BF16)
        v = v_ref[ks, :].astype(BF16)
        for c in range(2):
            s = _dot_nt(qm[c], k)
            if masked:
                s = jnp.where(diag_mask, s, -jnp.inf)
            m_old = m_scr[c]
            m_new = jnp.maximum(m_old, jnp.max(s, axis=-1, keepdims=True))
            alpha = jnp.exp(m_old - m_new)
            p = jnp.exp(s - m_new)
            l_scr[c] = alpha * l_scr[c] + jnp.sum(p, axis=-1, keepdims=True)
            acc_scr[c] = alpha * acc_scr[c] + _dot(p.astype(BF16), v)
            m_scr[c] = m_new

    def body(kj, carry):
        block(kj, False)
        return carry

    lax.fori_loop(0, qi, body, 0)
    block(qi, True)
    lam = _lambda(lq1, lk1, lq2, lk2, lam_init)
    _attn_finish(acc_scr[0] / l_scr[0], acc_scr[1] / l_scr[1], lam, sg_ref, z_ref, o_ref, lam_init)


def _attn_prompt(q3, k3, v3, z3, lq1, lk1, lq2, lk2, sg, lam_init, tq):
    bsz, tlen, hd = q3.shape
    nh = hd // LANES
    assert tlen % tq == 0 and tq % CHUNK == 0
    qblk = pl.BlockSpec((None, tq, LANES), lambda bi, hi, qi: (bi, qi, hi))
    kblk = pl.BlockSpec((None, tlen, LANES), lambda bi, hi, qi: (bi, 0, hi))
    vec = lambda n: pl.BlockSpec((1, n), lambda bi, hi, qi: (0, 0))
    return pl.pallas_call(
        functools.partial(_attn_prompt_kernel, tq=tq, lam_init=lam_init),
        grid=(bsz, nh, tlen // tq),
        in_specs=[qblk, kblk, kblk, qblk, vec(HC_DIM), vec(HC_DIM), vec(HC_DIM), vec(HC_DIM), vec(LANES)],
        out_specs=qblk,
        out_shape=jax.ShapeDtypeStruct((bsz, tlen, hd), F32),
        scratch_shapes=[pltpu.VMEM((2, tq, 1), F32), pltpu.VMEM((2, tq, 1), F32), pltpu.VMEM((2, tq, LANES), F32)],
        compiler_params=_params(("parallel", "parallel", "arbitrary")),
        name="attn_prompt",
    )(q3, k3, v3, z3, lq1, lk1, lq2, lk2, sg)


def _attn_sample_kernel(q_ref, ck_ref, cv_ref, kn_ref, vn_ref, z_ref, lq1, lk1, lq2, lk2, sg_ref, o_ref,
                        *, lam_init):
    tlen = q_ref.shape[0]
    past = ck_ref.shape[0]
    qm = _map_queries(q_ref)
    ck = ck_ref[...].astype(BF16)
    cv = cv_ref[...].astype(BF16)
    kn = kn_ref[...].astype(BF16)
    vn = vn_ref[...].astype(BF16)
    q_chunk = (past + lax.broadcasted_iota(jnp.int32, (tlen, 1), 0)) // CHUNK
    mask_c = lax.broadcasted_iota(jnp.int32, (tlen, past), 1) // CHUNK <= q_chunk
    mask_n = (past + lax.broadcasted_iota(jnp.int32, (tlen, tlen), 1)) // CHUNK <= q_chunk
    outs = []
    for c in range(2):
        sc = jnp.where(mask_c, _dot_nt(qm[c], ck), -jnp.inf)
        sn = jnp.where(mask_n, _dot_nt(qm[c], kn), -jnp.inf)
        m = jnp.maximum(jnp.max(sc, axis=-1, keepdims=True), jnp.max(sn, axis=-1, keepdims=True))
        pc = jnp.exp(sc - m)
        pn = jnp.exp(sn - m)
        l = jnp.sum(pc, axis=-1, keepdims=True) + jnp.sum(pn, axis=-1, keepdims=True)
        outs.append((_dot(pc.astype(BF16), cv) + _dot(pn.astype(BF16), vn)) / l)
    lam = _lambda(lq1, lk1, lq2, lk2, lam_init)
    _attn_finish(outs[0], outs[1], lam, sg_ref, z_ref, o_ref, lam_init)


def _attn_sample(q3, ck3, cv3, kn3, vn3, z3, lq1, lk1, lq2, lk2, sg, lam_init):
    bsz, tlen, hd = q3.shape
    past = ck3.shape[1]
    nh = hd // LANES
    new = pl.BlockSpec((None, tlen, LANES), lambda bi, hi: (bi, 0, hi))
    old = pl.BlockSpec((None, past, LANES), lambda bi, hi: (bi, 0, hi))
    vec = lambda n: pl.BlockSpec((1, n), lambda bi, hi: (0, 0))
    return pl.pallas_call(
        functools.partial(_attn_sample_kernel, lam_init=lam_init),
        grid=(bsz, nh),
        in_specs=[new, old, old, new, new, new, vec(HC_DIM), vec(HC_DIM), vec(HC_DIM), vec(HC_DIM), vec(LANES)],
        out_specs=new,
        out_shape=jax.ShapeDtypeStruct((bsz, tlen, hd), F32),
        compiler_params=_params(("parallel", "parallel")),
        name="attn_sample",
    )(q3, ck3, cv3, kn3, vn3, z3, lq1, lk1, lq2, lk2, sg)


def _row_tile(n, pref):
    while n % pref:
        pref //= 2
    return pref


def _even_layer(x3, conv_buf, s0, w_in, w_out, cw, cb, cg, cbb, lb_logits, hn_g, pg, pb, layer, alpha):
    bsz, tlen, d = x3.shape
    n = bsz * tlen
    x2 = x3.reshape(n, d)
    d_a = cw.shape[-1]
    h = _linear(x2, w_in, 0, w_in.shape[1], F32, _row_tile(n, 1024), 1024)
    h3 = h.reshape(bsz, tlen, -1)
    buf = jnp.pad(conv_buf, ((0, 0), (HIST - (CONV_W - 1), 0), (0, 0)))
    out_a, nbuf = _conv_branch(h3, buf, cw, cb, cg, cbb, _row_tile(tlen, 256))
    chunk = CHUNK if tlen % CHUNK == 0 else tlen
    tc = _row_tile(tlen, 512) if tlen % CHUNK == 0 else chunk
    out_b, s_new = _hgrn_branch(h3, lb_logits, hn_g, s0, layer, 3 * d_a // s0.shape[2] // s0.shape[1], tc, 1)
    xn = _out_ln(out_a.reshape(n, -1), 0, out_b.reshape(n, -1), 0, w_out, x2, pg, pb, alpha, _row_tile(n, 512))
    return xn.reshape(bsz, tlen, d), nbuf[:, HIST - (CONV_W - 1):], s_new


def _odd_layer(x3, k_past, v_past, w_in, w_out, lq1, lk1, lq2, lk2, sg, pg, pb, lam_init, alpha):
    bsz, tlen, d = x3.shape
    n = bsz * tlen
    x2 = x3.reshape(n, d)
    d_c = w_in.shape[1] // 4
    tm = _row_tile(n, 1024)
    q, k, v, z = [_linear(x2, w_in, i * d_c, d_c, F32, tm, 1024).reshape(bsz, tlen, d_c) for i in range(4)]
    if k_past is None:
        o = _attn_prompt(q, k, v, z, lq1, lk1, lq2, lk2, sg, lam_init, _row_tile(tlen, 512))
    else:
        past = k_past.shape[1]
        o = _attn_sample(q, k_past.reshape(bsz, past, d_c), v_past.reshape(bsz, past, d_c), k, v, z,
                         lq1, lk1, lq2, lk2, sg, lam_init)
    o2 = o.reshape(n, d_c)
    xn = _out_ln(o2, 0, o2, 1, w_out, x2, pg, pb, alpha, _row_tile(n, 512))
    nh = d_c // LANES
    return xn.reshape(bsz, tlen, d), k.reshape(bsz, tlen, nh, LANES), v.reshape(bsz, tlen, nh, LANES)


def kernel(x_prompt, x_sample, state_conv, state_hgrn, cache_k, cache_v, w_in_even, w_out_even, conv_w, conv_b,
           conv_ln_g, conv_ln_b, hgrn_lb_logits, hgrn_norm_g, w_in_odd, w_out_odd, lam_q1, lam_k1, lam_q2, lam_k2,
           subln_g, post_ln_g, post_ln_b):
    depth = post_ln_g.shape[0]
    alpha = (2.0 * depth) ** 0.25
    xp, xs = x_prompt, x_sample
    bp = xp.shape[0]
    row = lambda a: a.reshape(1, -1)
    outs = {name: [] for name in ("conv_p", "hgrn_p", "k_p", "v_p", "conv_s", "hgrn_s", "k_s", "v_s")}
    for l in range(depth):
        pg, pb = row(post_ln_g[l]), row(post_ln_b[l])
        if l % 2 == 0:
            e = l // 2
            prm = (w_in_even[e].astype(BF16), w_out_even[e].astype(BF16), conv_w[e], row(conv_b[e]),
                   row(conv_ln_g[e]), row(conv_ln_b[e]), hgrn_lb_logits, row(hgrn_norm_g[e]), pg, pb, e, alpha)
            buf0 = jnp.zeros((bp,) + state_conv.shape[2:], F32)
            s0 = jnp.zeros((bp,) + state_hgrn.shape[2:], F32)
            xp, cb_p, s_p = _even_layer(xp, buf0, s0, *prm)
            xs, cb_s, s_s = _even_layer(xs, state_conv[e], state_hgrn[e], *prm)
            outs["conv_p"].append(cb_p); outs["hgrn_p"].append(s_p)
            outs["conv_s"].append(cb_s); outs["hgrn_s"].append(s_s)
        else:
            o = l // 2
            lam_init = 0.8 - 0.6 * math.exp(-0.3 * l)
            prm = (w_in_odd[o].astype(BF16), w_out_odd[o].astype(BF16), row(lam_q1[o]), row(lam_k1[o]),
                   row(lam_q2[o]), row(lam_k2[o]), row(subln_g[o]), pg, pb, lam_init, alpha)
            xp, kn_p, vn_p = _odd_layer(xp, None, None, *prm)
            xs, kn_s, vn_s = _odd_layer(xs, cache_k[o], cache_v[o], *prm)
            outs["k_p"].append(kn_p); outs["v_p"].append(vn_p)
            outs["k_s"].append(kn_s); outs["v_s"].append(vn_s)
    st = lambda name: jnp.stack(outs[name])
    return (xp, xs, st("conv_p"), st("hgrn_p"), st("k_p"), st("v_p"),
            st("conv_s"), st("hgrn_s"), st("k_s"), st("v_s"))
```

```python
import functools
import math

import jax
import jax.numpy as jnp
from jax import lax
from jax.experimental import pallas as pl
from jax.experimental.pallas import tpu as pltpu

F32 = jnp.float32
BF16 = jnp.bfloat16

CHUNK = 64
SUBLANES = 8
CONV_W = 31
HIST = 32
HB_DK = 128
HC_DIM = 64
LANES = 128
EPS = 1e-5
LOG2E = math.log2(math.e)
HGRN_HEADS_PER_STEP = 8
VMEM_LIMIT = 48 * 1024 * 1024


def _params(sem):
    return pltpu.CompilerParams(dimension_semantics=sem, vmem_limit_bytes=VMEM_LIMIT)


def _dot(a, b):
    return jnp.dot(a, b, preferred_element_type=F32)


def _dot_nt(a, b):
    return lax.dot_general(a, b, (((1,), (1,)), ((), ())), preferred_element_type=F32)


def _dot_tn(a, b):
    return lax.dot_general(a, b, (((0,), (0,)), ((), ())), preferred_element_type=F32)


def _sigmoid(x):
    return 1.0 / (1.0 + jnp.exp(-x))


def _silu(x):
    return x * _sigmoid(x)


def _linear_kernel(x_ref, w_ref, o_ref):
    o_ref[...] = _dot(x_ref[...].astype(BF16), w_ref[...]).astype(o_ref.dtype)


def _linear(x, w, col0, ncols, out_dtype, tm, tn):
    n, k = x.shape
    assert n % tm == 0 and ncols % tn == 0 and col0 % tn == 0
    jb = col0 // tn
    return pl.pallas_call(
        _linear_kernel,
        grid=(n // tm, ncols // tn),
        in_specs=[pl.BlockSpec((tm, k), lambda i, j: (i, 0)),
                  pl.BlockSpec((k, tn), lambda i, j: (0, jb + j))],
        out_specs=pl.BlockSpec((tm, tn), lambda i, j: (i, j)),
        out_shape=jax.ShapeDtypeStruct((n, ncols), out_dtype),
        compiler_params=_params(("parallel", "arbitrary")),
        name="linear",
    )(x, w)


def _conv_kernel(a_ref, ga_ref, za_ref, buf_ref, w_ref, cb_ref, g_ref, b_ref, out_ref, nb_ref, scr, c_scr,
                 *, tt, rb):
    t = pl.program_id(1)
    nch = a_ref.shape[-1] // LANES

    @pl.when(t == 0)
    def _():
        scr[0:HIST, :] = buf_ref[...]

    @pl.when(t > 0)
    def _():
        scr[0:HIST, :] = scr[tt:tt + HIST, :]

    scr[HIST:HIST + tt, :] = a_ref[...] * _sigmoid(ga_ref[...])

    def row_block(r, carry):
        r0 = pl.multiple_of(r * rb, rb)
        for c in range(nch):
            cs = slice(c * LANES, (c + 1) * LANES)
            slab = scr[pl.ds(r0, rb + HIST), cs]
            acc = jnp.zeros((rb, LANES), F32)
            for j in range(CONV_W):
                acc = acc + w_ref[j:j + 1, cs] * slab[j + 2:j + 2 + rb, :]
            c_scr[pl.ds(r0, rb), cs] = acc + cb_ref[:, cs]
        return carry

    lax.fori_loop(0, tt // rb, row_block, 0)

    cv = c_scr[...]
    mu = jnp.mean(cv, axis=-1, keepdims=True)
    xc = cv - mu
    var = jnp.mean(xc * xc, axis=-1, keepdims=True)
    y = xc * lax.rsqrt(var + EPS) * g_ref[...] + b_ref[...]
    out_ref[...] = _silu(y) * _silu(za_ref[...])

    @pl.when(t == pl.num_programs(1) - 1)
    def _():
        nb_ref[...] = scr[tt:tt + HIST, :]


def _conv_branch(h3, buf, cw, cb, g, b, tt):
    bsz, tlen, _ = h3.shape
    c = cw.shape[-1]
    assert tlen % tt == 0
    rb = min(32, tt)
    assert tt % rb == 0
    row = lambda col: pl.BlockSpec((None, tt, c), lambda bi, ti: (bi, ti, col))
    vec = lambda r: pl.BlockSpec((r, c), lambda bi, ti: (0, 0))
    hist = pl.BlockSpec((None, HIST, c), lambda bi, ti: (bi, 0, 0))
    return pl.pallas_call(
        functools.partial(_conv_kernel, tt=tt, rb=rb),
        grid=(bsz, tlen // tt),
        in_specs=[row(0), row(1), row(2), hist, vec(CONV_W), vec(1), vec(1), vec(1)],
        out_specs=[pl.BlockSpec((None, tt, c), lambda bi, ti: (bi, ti, 0)), hist],
        out_shape=[jax.ShapeDtypeStruct((bsz, tlen, c), F32), jax.ShapeDtypeStruct((bsz, HIST, c), F32)],
        scratch_shapes=[pltpu.VMEM((tt + HIST, c), F32), pltpu.VMEM((tt, c), F32)],
        compiler_params=_params(("parallel", "arbitrary")),
        name="conv_branch",
    )(h3, h3, h3, buf, cw, cb, g, b)


def _cumsum_rows(tri3, g):
    g0 = g.astype(BF16)
    r1 = g - g0.astype(F32)
    g1 = r1.astype(BF16)
    g2 = (r1 - g1.astype(F32)).astype(BF16)
    return _dot(tri3, jnp.concatenate([g0, g1, g2], axis=0))


def _pair_reference(x, h):
    nrows, width = x.shape
    if h >= SUBLANES:
        parts = [jnp.broadcast_to(x[p + h - 1:p + h], (2 * h, width)) for p in range(0, nrows, 2 * h)]
        return parts[0] if len(parts) == 1 else jnp.concatenate(parts, axis=0)
    x3 = x.reshape(nrows // SUBLANES, SUBLANES, width)
    sub = lax.broadcasted_iota(jnp.int32, (1, SUBLANES, 1), 1)
    ref = None
    for p in range(SUBLANES - 2 * h, -1, -2 * h):
        row = x3[:, p + h - 1:p + h, :]
        ref = row if ref is None else jnp.where(sub < p + 2 * h, row, ref)
    return jnp.broadcast_to(ref, x3.shape).reshape(nrows, width)


def _hgrn_kernel(q_ref, f_ref, i_ref, zb_ref, lbl_ref, hg_ref, s0_ref, o_ref, s_ref, st_scr,
                 *, layer, chunk, nchunks, heads):
    t = pl.program_id(2)
    dk = HB_DK

    @pl.when(t == 0)
    def _():
        for h in range(heads):
            st_scr[h] = s0_ref[h].T

    lg = lbl_ref[...]
    ex = jnp.exp(lg - jnp.max(lg, axis=0, keepdims=True))
    sm = ex / jnp.sum(ex, axis=0, keepdims=True)
    lb = jnp.zeros_like(sm[0:1])
    for r in range(1, layer + 1):
        lb = lb + sm[r:r + 1]
    log_lb = jnp.log(lb)
    log1m_lb = jnp.log1p(-lb)

    rows = lax.broadcasted_iota(jnp.int32, (chunk, chunk), 0)
    cols = lax.broadcasted_iota(jnp.int32, (chunk, chunk), 1)
    tri = jnp.where(rows >= cols, 1.0, 0.0).astype(BF16)
    tri3 = jnp.concatenate([tri, tri, tri], axis=1)
    xor = jnp.bitwise_xor(rows, cols)
    halves = []
    h = chunk // 2
    while h >= 1:
        halves.append(h)
        h //= 2

    def one_chunk(ci, carry):
        r0 = pl.multiple_of(ci * chunk, chunk)
        rs = pl.ds(r0, chunk)
        for hd in range(heads):
            cs = slice(hd * dk, (hd + 1) * dk)
            fr = f_ref[rs, cs]
            lsig = jnp.minimum(fr, 0.0) - jnp.log(1.0 + jnp.exp(-jnp.abs(fr)))
            c2 = log1m_lb[:, cs] + lsig
            a2 = log_lb[:, cs]
            logf = jnp.maximum(a2, c2) + jnp.log(1.0 + jnp.exp(-jnp.abs(a2 - c2)))
            kk = (1.0 - lb[:, cs]) * _sigmoid(-fr)
            qq = _silu(q_ref[rs, cs])
            vv = i_ref[rs, cs]
            vb = vv.astype(BF16)
            bl = _cumsum_rows(tri3, logf) * LOG2E
            st = st_scr[hd]
            o = _dot_nt((qq * jnp.exp2(bl)).astype(BF16), st.astype(BF16))
            att = jnp.zeros((chunk, chunk), F32)
            for h in halves:
                fac = jnp.exp2(-jnp.abs(bl - _pair_reference(bl, h)))
                a_h = _dot_nt((qq * fac).astype(BF16), (kk * fac).astype(BF16))
                att = jnp.where((xor >= h) & (xor < 2 * h) & (rows > cols), a_h, att)
            o = o + _dot(att.astype(BF16), vb) + jnp.sum(qq * kk, axis=-1, keepdims=True) * vv
            bl_last = bl[chunk - 1:chunk]
            kd = kk * jnp.exp2(bl_last - bl)
            st_scr[hd] = st * jnp.exp2(bl_last) + _dot_tn(vb, kd.astype(BF16))
            on = o * lax.rsqrt(jnp.mean(o * o, axis=-1, keepdims=True) + EPS) * hg_ref[...]
            o_ref[rs, cs] = on * _silu(zb_ref[rs, cs])
        return carry

    lax.fori_loop(0, nchunks, one_chunk, 0)

    @pl.when(t == pl.num_programs(2) - 1)
    def _():
        for h in range(heads):
            s_ref[h] = st_scr[h].T


def _hgrn_branch(h3, lb_logits, hn_g, s0, layer, col_q, tc, heads):
    bsz, tlen, _ = h3.shape
    nh, dk, dv = s0.shape[1:]
    d_b = nh * dk
    chunk = CHUNK if tlen % CHUNK == 0 else tlen
    assert chunk % SUBLANES == 0 and chunk & (chunk - 1) == 0
    assert tc % chunk == 0 and tlen % tc == 0 and nh % heads == 0
    w = heads * dk
    gpb = d_b // w
    col = lambda k: pl.BlockSpec((None, tc, w), lambda bi, hi, ti: (bi, ti, (col_q + k) * gpb + hi))
    st = pl.BlockSpec((None, heads, dk, dv), lambda bi, hi, ti: (bi, hi, 0, 0))
    ne = lb_logits.shape[0]
    return pl.pallas_call(
        functools.partial(_hgrn_kernel, layer=layer, chunk=chunk, nchunks=tc // chunk, heads=heads),
        grid=(bsz, nh // heads, tlen // tc),
        in_specs=[col(0), col(1), col(2), col(3),
                  pl.BlockSpec((ne, w), lambda bi, hi, ti: (0, hi)),
                  pl.BlockSpec((1, dv), lambda bi, hi, ti: (0, 0)),
                  st],
        out_specs=[pl.BlockSpec((None, tc, w), lambda bi, hi, ti: (bi, ti, hi)), st],
        out_shape=[jax.ShapeDtypeStruct((bsz, tlen, d_b), F32), jax.ShapeDtypeStruct(s0.shape, F32)],
        scratch_shapes=[pltpu.VMEM((heads, dv, dk), F32)],
        compiler_params=_params(("parallel", "parallel", "arbitrary")),
        name="hgrn_branch",
    )(h3, h3, h3, h3, lb_logits, hn_g, s0)


def _out_ln_kernel(a_ref, b_ref, w_ref, x_ref, g_ref, bb_ref, o_ref, *, alpha):
    ka = a_ref.shape[-1]
    y = _dot(a_ref[...].astype(BF16), w_ref[0:ka, :]) + _dot(b_ref[...].astype(BF16), w_ref[ka:, :])
    r = alpha * x_ref[...] + y
    mu = jnp.mean(r, axis=-1, keepdims=True)
    rc = r - mu
    var = jnp.mean(rc * rc, axis=-1, keepdims=True)
    o_ref[...] = rc * lax.rsqrt(var + EPS) * g_ref[...] + bb_ref[...]


def _out_ln(a, a_col, b, b_col, w, x, g, bb, alpha, tm):
    n, d = x.shape
    kh = w.shape[0] // 2
    assert n % tm == 0
    return pl.pallas_call(
        functools.partial(_out_ln_kernel, alpha=alpha),
        grid=(n // tm,),
        in_specs=[pl.BlockSpec((tm, kh), lambda i: (i, a_col)),
                  pl.BlockSpec((tm, kh), lambda i: (i, b_col)),
                  pl.BlockSpec(w.shape, lambda i: (0, 0)),
                  pl.BlockSpec((tm, d), lambda i: (i, 0)),
                  pl.BlockSpec((1, d), lambda i: (0, 0)),
                  pl.BlockSpec((1, d), lambda i: (0, 0))],
        out_specs=pl.BlockSpec((tm, d), lambda i: (i, 0)),
        out_shape=jax.ShapeDtypeStruct((n, d), F32),
        compiler_params=_params(("parallel",)),
        name="out_ln",
    )(a, b, w, x, g, bb)


def _lambda(lq1, lk1, lq2, lk2, lam_init):
    e1 = jnp.exp(jnp.sum(lq1[...] * lk1[...], axis=-1, keepdims=True))
    e2 = jnp.exp(jnp.sum(lq2[...] * lk2[...], axis=-1, keepdims=True))
    return e1 - e2 + lam_init


def _map_queries(q_ref, scale):
    q = q_ref[...] * scale
    lane = lax.broadcasted_iota(jnp.int32, q.shape, 1)
    return jnp.concatenate([jnp.where(lane < HC_DIM, q, 0.0), jnp.where(lane >= HC_DIM, q, 0.0)],
                           axis=0).astype(BF16)


def _attn_finish(o1, o2, lam, sg_ref, z_ref, o_ref, lam_init):
    o = o1 - lam * o2
    o = o * lax.rsqrt(jnp.mean(o * o, axis=-1, keepdims=True) + EPS) * sg_ref[...] * (1.0 - lam_init)
    o_ref[...] = o * _silu(z_ref[...])


def _attn_prompt_kernel(q_ref, k_ref, v_ref, z_ref, lq1, lk1, lq2, lk2, sg_ref, o_ref, kb_scr, vb_scr, m_scr, acc_scr, s_scr,
                        *, tq, tk, lam_init):
    qi = pl.program_id(2)
    tlen = k_ref.shape[0]
    nlt = tk // LANES

    @pl.when(qi == 0)
    def _():
        kb_scr[...] = k_ref[...].astype(BF16)
        vb_scr[:, 0:LANES] = v_ref[...].astype(BF16)
        vb_scr[:, LANES:] = jnp.ones((tlen, LANES), BF16)

    q2 = _map_queries(q_ref, (HC_DIM ** -0.5) * LOG2E)
    m_scr[...] = jnp.full(m_scr.shape, -jnp.inf, F32)
    acc_scr[...] = jnp.zeros(acc_scr.shape, F32)
    q0 = qi * tq

    def scores(kj):
        k0 = pl.multiple_of(kj * tk, tk)
        return _dot_nt(q2, kb_scr[pl.ds(k0, tk), :])

    def block(kj, masked):
        k0 = pl.multiple_of(kj * tk, tk)
        s = s_scr[...]
        if not masked:
            s_scr[...] = scores(kj + 1)
        if masked:
            r = lax.broadcasted_iota(jnp.int32, (2 * tq, 1), 0)
            q_chunk = lax.div(q0 + jnp.where(r >= tq, r - tq, r), CHUNK)
            k_chunk = lax.div(k0 + lax.broadcasted_iota(jnp.int32, (1, tk), 1), CHUNK)
            s = jnp.where(k_chunk <= q_chunk, s, -jnp.inf)
        tiles = [s[:, t * LANES:(t + 1) * LANES] for t in range(nlt)]
        m_cur = tiles[0]
        for t in range(1, nlt):
            m_cur = jnp.maximum(m_cur, tiles[t])
        m_old = m_scr[...]
        m_new = jnp.maximum(m_old, jnp.max(m_cur, axis=-1, keepdims=True))
        alpha = jnp.exp2(m_old - m_new)
        p = jnp.concatenate([jnp.exp2(tl - m_new).astype(BF16) for tl in tiles], axis=1)
        acc_scr[...] = (jnp.concatenate([alpha, alpha], axis=1) * acc_scr[...]
                        + _dot(p, vb_scr[pl.ds(k0, tk), :]))
        m_scr[...] = m_new

    def body(kj, carry):
        block(kj, False)
        return carry

    nfull = q0 // tk
    s_scr[...] = scores(0)
    lax.fori_loop(0, nfull, body, 0)
    block(nfull, True)
    lam = _lambda(lq1, lk1, lq2, lk2, lam_init)
    acc = acc_scr[...]
    _attn_finish(acc[0:tq, 0:LANES] / acc[0:tq, LANES:], acc[tq:, 0:LANES] / acc[tq:, LANES:],
                 lam, sg_ref, z_ref, o_ref, lam_init)


def _attn_prompt(q3, k3, v3, z3, lq1, lk1, lq2, lk2, sg, lam_init, tq, tk):
    bsz, tlen, hd = q3.shape
    nh = hd // LANES
    assert tlen % tk == 0 and tk % tq == 0 and tq % CHUNK == 0
    qblk = pl.BlockSpec((None, tq, LANES), lambda bi, hi, qi: (bi, qi, hi))
    kblk = pl.BlockSpec((None, tlen, LANES), lambda bi, hi, qi: (bi, 0, hi))
    vec = lambda n: pl.BlockSpec((1, n), lambda bi, hi, qi: (0, 0))
    return pl.pallas_call(
        functools.partial(_attn_prompt_kernel, tq=tq, tk=tk, lam_init=lam_init),
        grid=(bsz, nh, tlen // tq),
        in_specs=[qblk, kblk, kblk, qblk, vec(HC_DIM), vec(HC_DIM), vec(HC_DIM), vec(HC_DIM), vec(LANES)],
        out_specs=qblk,
        out_shape=jax.ShapeDtypeStruct((bsz, tlen, hd), F32),
        scratch_shapes=[pltpu.VMEM((tlen, LANES), BF16), pltpu.VMEM((tlen, 2 * LANES), BF16),
                        pltpu.VMEM((2 * tq, LANES), F32), pltpu.VMEM((2 * tq, 2 * LANES), F32),
                        pltpu.VMEM((2 * tq, tk), F32)],
        compiler_params=_params(("parallel", "parallel", "arbitrary")),
        name="attn_prompt",
    )(q3, k3, v3, z3, lq1, lk1, lq2, lk2, sg)


def _attn_sample_kernel(q_ref, ck_ref, cv_ref, kn_ref, vn_ref, z_ref, lq1, lk1, lq2, lk2, sg_ref, o_ref,
                        *, lam_init):
    tlen = q_ref.shape[0]
    past = ck_ref.shape[0]
    q2 = _map_queries(q_ref, (HC_DIM ** -0.5) * LOG2E)
    ck = ck_ref[...].astype(BF16)
    cv = cv_ref[...].astype(BF16)
    kn = kn_ref[...].astype(BF16)
    vn = vn_ref[...].astype(BF16)
    r = lax.broadcasted_iota(jnp.int32, (2 * tlen, 1), 0)
    q_chunk = lax.div(past + jnp.where(r >= tlen, r - tlen, r), CHUNK)
    mask_c = lax.div(lax.broadcasted_iota(jnp.int32, (1, past), 1), CHUNK) <= q_chunk
    mask_n = lax.div(past + lax.broadcasted_iota(jnp.int32, (1, tlen), 1), CHUNK) <= q_chunk
    sc = jnp.where(mask_c, _dot_nt(q2, ck), -jnp.inf)
    sn = jnp.where(mask_n, _dot_nt(q2, kn), -jnp.inf)
    m = jnp.maximum(jnp.max(sc, axis=-1, keepdims=True), jnp.max(sn, axis=-1, keepdims=True))
    pc = jnp.exp2(sc - m)
    pn = jnp.exp2(sn - m)
    l = jnp.sum(pc, axis=-1, keepdims=True) + jnp.sum(pn, axis=-1, keepdims=True)
    o = (_dot(pc.astype(BF16), cv) + _dot(pn.astype(BF16), vn)) / l
    lam = _lambda(lq1, lk1, lq2, lk2, lam_init)
    _attn_finish(o[0:tlen], o[tlen:], lam, sg_ref, z_ref, o_ref, lam_init)


def _attn_sample(q3, ck4, cv4, layer, kn3, vn3, z3, lq1, lk1, lq2, lk2, sg, lam_init):
    bsz, tlen, hd = q3.shape
    past = ck4.shape[2]
    nh = hd // LANES
    new = pl.BlockSpec((None, tlen, LANES), lambda bi, hi: (bi, 0, hi))
    old = pl.BlockSpec((None, None, past, LANES), lambda bi, hi: (layer, bi, 0, hi))
    vec = lambda n: pl.BlockSpec((1, n), lambda bi, hi: (0, 0))
    return pl.pallas_call(
        functools.partial(_attn_sample_kernel, lam_init=lam_init),
        grid=(bsz, nh),
        in_specs=[new, old, old, new, new, new, vec(HC_DIM), vec(HC_DIM), vec(HC_DIM), vec(HC_DIM), vec(LANES)],
        out_specs=new,
        out_shape=jax.ShapeDtypeStruct((bsz, tlen, hd), F32),
        compiler_params=_params(("parallel", "parallel")),
        name="attn_sample",
    )(q3, ck4, cv4, kn3, vn3, z3, lq1, lk1, lq2, lk2, sg)


def _row_tile(n, pref):
    while n % pref:
        pref //= 2
    return pref


def _even_layer(x3, conv_buf, s0, w_in, w_out, cw, cb, cg, cbb, lb_logits, hn_g, pg, pb, layer, alpha):
    bsz, tlen, d = x3.shape
    n = bsz * tlen
    x2 = x3.reshape(n, d)
    d_a = cw.shape[-1]
    h = _linear(x2, w_in, 0, w_in.shape[1], F32, _row_tile(n, 1024), 1024)
    h3 = h.reshape(bsz, tlen, -1)
    buf = jnp.pad(conv_buf, ((0, 0), (HIST - (CONV_W - 1), 0), (0, 0)))
    out_a, nbuf = _conv_branch(h3, buf, cw, cb, cg, cbb, _row_tile(tlen, 256))
    chunk = CHUNK if tlen % CHUNK == 0 else tlen
    tc = _row_tile(tlen, 512) if tlen % CHUNK == 0 else chunk
    d_b = s0.shape[1] * s0.shape[2]
    out_b, s_new = _hgrn_branch(h3, lb_logits, hn_g, s0, layer, 3 * d_a // d_b, tc, HGRN_HEADS_PER_STEP)
    xn = _out_ln(out_a.reshape(n, -1), 0, out_b.reshape(n, -1), 0, w_out, x2, pg, pb, alpha, _row_tile(n, 512))
    return xn.reshape(bsz, tlen, d), nbuf[:, HIST - (CONV_W - 1):], s_new


def _odd_layer(x3, caches, w_in, w_out, lq1, lk1, lq2, lk2, sg, pg, pb, lam_init, alpha):
    bsz, tlen, d = x3.shape
    n = bsz * tlen
    x2 = x3.reshape(n, d)
    d_c = w_in.shape[1] // 4
    tm = _row_tile(n, 1024)
    q, k, v, z = [_linear(x2, w_in, i * d_c, d_c, F32, tm, 1024).reshape(bsz, tlen, d_c) for i in range(4)]
    if caches is None:
        tk = _row_tile(tlen, 512)
        o = _attn_prompt(q, k, v, z, lq1, lk1, lq2, lk2, sg, lam_init, tk, tk)
    else:
        ck, cv, layer = caches
        flat = ck.shape[:3] + (d_c,)
        o = _attn_sample(q, ck.reshape(flat), cv.reshape(flat), layer, k, v, z, lq1, lk1, lq2, lk2, sg, lam_init)
    o2 = o.reshape(n, d_c)
    xn = _out_ln(o2, 0, o2, 1, w_out, x2, pg, pb, alpha, _row_tile(n, 512))
    nh = d_c // LANES
    return xn.reshape(bsz, tlen, d), k.reshape(bsz, tlen, nh, LANES), v.reshape(bsz, tlen, nh, LANES)


def kernel(x_prompt, x_sample, state_conv, state_hgrn, cache_k, cache_v, w_in_even, w_out_even, conv_w, conv_b,
           conv_ln_g, conv_ln_b, hgrn_lb_logits, hgrn_norm_g, w_in_odd, w_out_odd, lam_q1, lam_k1, lam_q2, lam_k2,
           subln_g, post_ln_g, post_ln_b):
    depth = post_ln_g.shape[0]
    alpha = (2.0 * depth) ** 0.25
    xp, xs = x_prompt, x_sample
    bp = xp.shape[0]
    row = lambda a: a.reshape(1, -1)
    outs = {name: [] for name in ("conv_p", "hgrn_p", "k_p", "v_p", "conv_s", "hgrn_s", "k_s", "v_s")}
    for l in range(depth):
        pg, pb = row(post_ln_g[l]), row(post_ln_b[l])
        if l % 2 == 0:
            e = l // 2
            prm = (w_in_even[e].astype(BF16), w_out_even[e].astype(BF16), conv_w[e], row(conv_b[e]),
                   row(conv_ln_g[e]), row(conv_ln_b[e]), hgrn_lb_logits, row(hgrn_norm_g[e]), pg, pb, e, alpha)
            buf0 = jnp.zeros((bp,) + state_conv.shape[2:], F32)
            s0 = jnp.zeros((bp,) + state_hgrn.shape[2:], F32)
            xp, cb_p, s_p = _even_layer(xp, buf0, s0, *prm)
            xs, cb_s, s_s = _even_layer(xs, state_conv[e], state_hgrn[e], *prm)
            outs["conv_p"].append(cb_p); outs["hgrn_p"].append(s_p)
            outs["conv_s"].append(cb_s); outs["hgrn_s"].append(s_s)
        else:
            o = l // 2
            lam_init = 0.8 - 0.6 * math.exp(-0.3 * l)
            prm = (w_in_odd[o].astype(BF16), w_out_odd[o].astype(BF16), row(lam_q1[o]), row(lam_k1[o]),
                   row(lam_q2[o]), row(lam_k2[o]), row(subln_g[o]), pg, pb, lam_init, alpha)
            xp, kn_p, vn_p = _odd_layer(xp, None, *prm)
            xs, kn_s, vn_s = _odd_layer(xs, (cache_k, cache_v, o), *prm)
            outs["k_p"].append(kn_p); outs["v_p"].append(vn_p)
            outs["k_s"].append(kn_s); outs["v_s"].append(vn_s)
    st = lambda name: jnp.stack(outs[name])
    return (xp, xs, st("conv_p"), st("hgrn_p"), st("k_p"), st("v_p"),
            st("conv_s"), st("hgrn_s"), st("k_s"), st("v_s"))
```

```python
import functools
import math

import jax
import jax.numpy as jnp
from jax import lax
from jax.experimental import pallas as pl
from jax.experimental.pallas import tpu as pltpu

F32 = jnp.float32
BF16 = jnp.bfloat16

CHUNK = 64
SUBLANES = 8
CONV_W = 31
HIST = 32
HB_DK = 128
HC_DIM = 64
LANES = 128
EPS = 1e-5
LOG2E = math.log2(math.e)
HGRN_HEADS_PER_STEP = 8
VMEM_LIMIT = 48 * 1024 * 1024


def _params(sem):
    return pltpu.CompilerParams(dimension_semantics=sem, vmem_limit_bytes=VMEM_LIMIT)


def _dot(a, b):
    return jnp.dot(a, b, preferred_element_type=F32)


def _dot_nt(a, b):
    return lax.dot_general(a, b, (((1,), (1,)), ((), ())), preferred_element_type=F32)


def _dot_tn(a, b):
    return lax.dot_general(a, b, (((0,), (0,)), ((), ())), preferred_element_type=F32)


def _sigmoid(x):
    return 1.0 / (1.0 + jnp.exp(-x))


def _silu(x):
    return x * _sigmoid(x)


def _linear_kernel(x_ref, w_ref, o_ref):
    o_ref[...] = _dot(x_ref[...].astype(BF16), w_ref[...]).astype(o_ref.dtype)


def _linear(x, w, col0, ncols, out_dtype, tm, tn):
    n, k = x.shape
    assert n % tm == 0 and ncols % tn == 0 and col0 % tn == 0
    jb = col0 // tn
    return pl.pallas_call(
        _linear_kernel,
        grid=(n // tm, ncols // tn),
        in_specs=[pl.BlockSpec((tm, k), lambda i, j: (i, 0)),
                  pl.BlockSpec((k, tn), lambda i, j: (0, jb + j))],
        out_specs=pl.BlockSpec((tm, tn), lambda i, j: (i, j)),
        out_shape=jax.ShapeDtypeStruct((n, ncols), out_dtype),
        compiler_params=_params(("parallel", "arbitrary")),
        name="linear",
    )(x, w)


def _conv_kernel(a_ref, ga_ref, za_ref, buf_ref, w_ref, cb_ref, g_ref, b_ref, out_ref, nb_ref, scr, c_scr, sh_scr,
                 *, tt, rb):
    t = pl.program_id(1)
    nch = a_ref.shape[-1] // LANES

    @pl.when(t == 0)
    def _():
        scr[0:HIST, :] = buf_ref[...]

    @pl.when(t > 0)
    def _():
        scr[0:HIST, :] = scr[tt:tt + HIST, :]

    scr[HIST:HIST + tt, :] = a_ref[...] * _sigmoid(ga_ref[...])

    def row_block(r, carry):
        r0 = pl.multiple_of(r * rb, rb)
        for c in range(nch):
            cs = slice(c * LANES, (c + 1) * LANES)
            slab = scr[pl.ds(r0, rb + HIST), cs]
            for shift in range(SUBLANES):
                n = (rb + HIST - shift) // SUBLANES * SUBLANES
                sh_scr[shift, 0:n, :] = slab[shift:shift + n, :]
            acc = jnp.zeros((rb, LANES), F32)
            for j in range(CONV_W):
                shift = (j + 2) % SUBLANES
                off = j + 2 - shift
                acc = acc + w_ref[j:j + 1, cs] * sh_scr[shift, off:off + rb, :]
            c_scr[pl.ds(r0, rb), cs] = acc + cb_ref[:, cs]
        return carry

    lax.fori_loop(0, tt // rb, row_block, 0)

    cv = c_scr[...]
    mu = jnp.mean(cv, axis=-1, keepdims=True)
    xc = cv - mu
    var = jnp.mean(xc * xc, axis=-1, keepdims=True)
    y = xc * lax.rsqrt(var + EPS) * g_ref[...] + b_ref[...]
    out_ref[...] = _silu(y) * _silu(za_ref[...])

    @pl.when(t == pl.num_programs(1) - 1)
    def _():
        nb_ref[...] = scr[tt:tt + HIST, :]


def _conv_branch(h3, buf, cw, cb, g, b, tt):
    bsz, tlen, _ = h3.shape
    c = cw.shape[-1]
    assert tlen % tt == 0
    rb = min(64, tt)
    assert tt % rb == 0
    row = lambda col: pl.BlockSpec((None, tt, c), lambda bi, ti: (bi, ti, col))
    vec = lambda r: pl.BlockSpec((r, c), lambda bi, ti: (0, 0))
    hist = pl.BlockSpec((None, HIST, c), lambda bi, ti: (bi, 0, 0))
    return pl.pallas_call(
        functools.partial(_conv_kernel, tt=tt, rb=rb),
        grid=(bsz, tlen // tt),
        in_specs=[row(0), row(1), row(2), hist, vec(CONV_W), vec(1), vec(1), vec(1)],
        out_specs=[pl.BlockSpec((None, tt, c), lambda bi, ti: (bi, ti, 0)), hist],
        out_shape=[jax.ShapeDtypeStruct((bsz, tlen, c), F32), jax.ShapeDtypeStruct((bsz, HIST, c), F32)],
        scratch_shapes=[pltpu.VMEM((tt + HIST, c), F32), pltpu.VMEM((tt, c), F32),
                        pltpu.VMEM((SUBLANES, rb + HIST, LANES), F32)],
        compiler_params=_params(("parallel", "arbitrary")),
        name="conv_branch",
    )(h3, h3, h3, buf, cw, cb, g, b)


def _cumsum_rows(tri3, g):
    g0 = g.astype(BF16)
    r1 = g - g0.astype(F32)
    g1 = r1.astype(BF16)
    g2 = (r1 - g1.astype(F32)).astype(BF16)
    return _dot(tri3, jnp.concatenate([g0, g1, g2], axis=0))


def _pair_reference(x, h):
    nrows, width = x.shape
    if h >= SUBLANES:
        parts = [jnp.broadcast_to(x[p + h - 1:p + h], (2 * h, width)) for p in range(0, nrows, 2 * h)]
        return parts[0] if len(parts) == 1 else jnp.concatenate(parts, axis=0)
    x3 = x.reshape(nrows // SUBLANES, SUBLANES, width)
    sub = lax.broadcasted_iota(jnp.int32, (1, SUBLANES, 1), 1)
    ref = None
    for p in range(SUBLANES - 2 * h, -1, -2 * h):
        row = x3[:, p + h - 1:p + h, :]
        ref = row if ref is None else jnp.where(sub < p + 2 * h, row, ref)
    return jnp.broadcast_to(ref, x3.shape).reshape(nrows, width)


def _hgrn_kernel(q_ref, f_ref, i_ref, zb_ref, lbl_ref, hg_ref, s0_ref, o_ref, s_ref, st_scr,
                 *, layer, chunk, nchunks, heads):
    t = pl.program_id(2)
    dk = HB_DK

    @pl.when(t == 0)
    def _():
        for h in range(heads):
            st_scr[h] = s0_ref[h].T

    lg = lbl_ref[...]
    ex = jnp.exp(lg - jnp.max(lg, axis=0, keepdims=True))
    sm = ex / jnp.sum(ex, axis=0, keepdims=True)
    lb = jnp.zeros_like(sm[0:1])
    for r in range(1, layer + 1):
        lb = lb + sm[r:r + 1]
    log_lb = jnp.log(lb)
    log1m_lb = jnp.log1p(-lb)

    rows = lax.broadcasted_iota(jnp.int32, (chunk, chunk), 0)
    cols = lax.broadcasted_iota(jnp.int32, (chunk, chunk), 1)
    tri = jnp.where(rows >= cols, 1.0, 0.0).astype(BF16)
    tri3 = jnp.concatenate([tri, tri, tri], axis=1)
    rows2 = lax.broadcasted_iota(jnp.int32, (chunk, 2 * chunk), 0)
    cols2 = lax.rem(lax.broadcasted_iota(jnp.int32, (chunk, 2 * chunk), 1), chunk)
    xor2 = jnp.bitwise_xor(rows2, cols2)
    levels = [0]
    h = 1
    while h < chunk:
        levels.append(h)
        h *= 2
    heads_cols = [slice(hd * dk, (hd + 1) * dk) for hd in range(heads)]
    zero_blk = jnp.zeros((chunk, dk), BF16)

    def block_diag(x, g):
        a, b = x[:, heads_cols[2 * g]], x[:, heads_cols[2 * g + 1]]
        return jnp.concatenate([jnp.concatenate([a, zero_blk], axis=1), jnp.concatenate([zero_blk, b], axis=1)],
                               axis=0)

    def one_chunk(ci, carry):
        r0 = pl.multiple_of(ci * chunk, chunk)
        rs = pl.ds(r0, chunk)
        fr = f_ref[rs, :]
        lsig = jnp.minimum(fr, 0.0) - jnp.log(1.0 + jnp.exp(-jnp.abs(fr)))
        c2 = log1m_lb + lsig
        logf = jnp.maximum(log_lb, c2) + jnp.log(1.0 + jnp.exp(-jnp.abs(log_lb - c2)))
        kk = (1.0 - lb) * _sigmoid(-fr)
        qq = _silu(q_ref[rs, :])
        vv = i_ref[rs, :]
        vb = vv.astype(BF16)
        bl = _cumsum_rows(tri3, logf) * LOG2E
        qe = (qq * jnp.exp2(bl)).astype(BF16)
        o = jnp.concatenate([_dot_nt(qe[:, cs], st_scr[hd].astype(BF16)) for hd, cs in enumerate(heads_cols)],
                            axis=1)
        att = [jnp.zeros((chunk, 2 * chunk), F32) for _ in range(heads // 2)]
        for h in levels:
            if h == 0:
                qf, kf, keep = qq.astype(BF16), kk.astype(BF16), rows2 == cols2
            else:
                fac = jnp.exp2(-jnp.abs(bl - _pair_reference(bl, h)))
                qf, kf = (qq * fac).astype(BF16), (kk * fac).astype(BF16)
                keep = (xor2 >= h) & (xor2 < 2 * h) & (rows2 > cols2)
            for g in range(heads // 2):
                a_h = _dot_nt(qf[:, 2 * g * dk:(2 * g + 2) * dk], block_diag(kf, g))
                att[g] = jnp.where(keep, a_h, att[g])
        o = o + jnp.concatenate([_dot(att[g].astype(BF16), block_diag(vb, g)) for g in range(heads // 2)], axis=1)
        bl_last = bl[chunk - 1:chunk]
        kd = (kk * jnp.exp2(bl_last - bl)).astype(BF16)
        decay = jnp.exp2(bl_last)
        for hd, cs in enumerate(heads_cols):
            st_scr[hd] = st_scr[hd] * decay[:, cs] + _dot_tn(vb[:, cs], kd[:, cs])
        inv = jnp.concatenate(
            [jnp.broadcast_to(lax.rsqrt(jnp.mean(o[:, cs] * o[:, cs], axis=-1, keepdims=True) + EPS), (chunk, dk))
             for cs in heads_cols], axis=1)
        o_ref[rs, :] = o * inv * jnp.concatenate([hg_ref[...]] * heads, axis=1) * _silu(zb_ref[rs, :])
        return carry

    lax.fori_loop(0, nchunks, one_chunk, 0)

    @pl.when(t == pl.num_programs(2) - 1)
    def _():
        for h in range(heads):
            s_ref[h] = st_scr[h].T


def _hgrn_branch(h3, lb_logits, hn_g, s0, layer, col_q, tc, heads):
    bsz, tlen, _ = h3.shape
    nh, dk, dv = s0.shape[1:]
    d_b = nh * dk
    chunk = CHUNK if tlen % CHUNK == 0 else tlen
    assert chunk % SUBLANES == 0 and chunk & (chunk - 1) == 0
    assert tc % chunk == 0 and tlen % tc == 0 and nh % heads == 0
    w = heads * dk
    gpb = d_b // w
    col = lambda k: pl.BlockSpec((None, tc, w), lambda bi, hi, ti: (bi, ti, (col_q + k) * gpb + hi))
    st = pl.BlockSpec((None, heads, dk, dv), lambda bi, hi, ti: (bi, hi, 0, 0))
    ne = lb_logits.shape[0]
    return pl.pallas_call(
        functools.partial(_hgrn_kernel, layer=layer, chunk=chunk, nchunks=tc // chunk, heads=heads),
        grid=(bsz, nh // heads, tlen // tc),
        in_specs=[col(0), col(1), col(2), col(3),
                  pl.BlockSpec((ne, w), lambda bi, hi, ti: (0, hi)),
                  pl.BlockSpec((1, dv), lambda bi, hi, ti: (0, 0)),
                  st],
        out_specs=[pl.BlockSpec((None, tc, w), lambda bi, hi, ti: (bi, ti, hi)), st],
        out_shape=[jax.ShapeDtypeStruct((bsz, tlen, d_b), F32), jax.ShapeDtypeStruct(s0.shape, F32)],
        scratch_shapes=[pltpu.VMEM((heads, dv, dk), F32)],
        compiler_params=_params(("parallel", "parallel", "arbitrary")),
        name="hgrn_branch",
    )(h3, h3, h3, h3, lb_logits, hn_g, s0)


def _out_ln_kernel(a_ref, b_ref, w_ref, x_ref, g_ref, bb_ref, o_ref, *, alpha):
    ka = a_ref.shape[-1]
    y = _dot(a_ref[...].astype(BF16), w_ref[0:ka, :]) + _dot(b_ref[...].astype(BF16), w_ref[ka:, :])
    r = alpha * x_ref[...] + y
    mu = jnp.mean(r, axis=-1, keepdims=True)
    rc = r - mu
    var = jnp.mean(rc * rc, axis=-1, keepdims=True)
    o_ref[...] = rc * lax.rsqrt(var + EPS) * g_ref[...] + bb_ref[...]


def _out_ln(a, a_col, b, b_col, w, x, g, bb, alpha, tm):
    n, d = x.shape
    kh = w.shape[0] // 2
    assert n % tm == 0
    return pl.pallas_call(
        functools.partial(_out_ln_kernel, alpha=alpha),
        grid=(n // tm,),
        in_specs=[pl.BlockSpec((tm, kh), lambda i: (i, a_col)),
                  pl.BlockSpec((tm, kh), lambda i: (i, b_col)),
                  pl.BlockSpec(w.shape, lambda i: (0, 0)),
                  pl.BlockSpec((tm, d), lambda i: (i, 0)),
                  pl.BlockSpec((1, d), lambda i: (0, 0)),
                  pl.BlockSpec((1, d), lambda i: (0, 0))],
        out_specs=pl.BlockSpec((tm, d), lambda i: (i, 0)),
        out_shape=jax.ShapeDtypeStruct((n, d), F32),
        compiler_params=_params(("parallel",)),
        name="out_ln",
    )(a, b, w, x, g, bb)


def _lambda(lq1, lk1, lq2, lk2, lam_init):
    e1 = jnp.exp(jnp.sum(lq1[...] * lk1[...], axis=-1, keepdims=True))
    e2 = jnp.exp(jnp.sum(lq2[...] * lk2[...], axis=-1, keepdims=True))
    return e1 - e2 + lam_init


def _map_queries(q, scale):
    q = q * scale
    lane = lax.broadcasted_iota(jnp.int32, q.shape, 1)
    return jnp.concatenate([jnp.where(lane < HC_DIM, q, 0.0), jnp.where(lane >= HC_DIM, q, 0.0)],
                           axis=0).astype(BF16)


def _attn_finish(o1, o2, lam, sg_ref, z, o_ref, cols, lam_init):
    o = o1 - lam * o2
    o = o * lax.rsqrt(jnp.mean(o * o, axis=-1, keepdims=True) + EPS) * sg_ref[...] * (1.0 - lam_init)
    o_ref[:, cols] = o * _silu(z)


def _attn_prompt_kernel(q_ref, k_ref, v_ref, z_ref, lq1, lk1, lq2, lk2, sg_ref, o_ref, kb_scr, vb_scr, m_scr, acc_scr, s_scr,
                        *, tq, tk, lam_init):
    qi = pl.program_id(2)
    tlen = k_ref.shape[0]
    nlt = tk // LANES

    @pl.when(qi == 0)
    def _():
        kb_scr[...] = k_ref[...].astype(BF16)
        vb_scr[:, 0:LANES] = v_ref[...].astype(BF16)
        vb_scr[:, LANES:] = jnp.ones((tlen, LANES), BF16)

    q2 = _map_queries(q_ref[...], (HC_DIM ** -0.5) * LOG2E)
    m_scr[...] = jnp.full(m_scr.shape, -jnp.inf, F32)
    acc_scr[...] = jnp.zeros(acc_scr.shape, F32)
    q0 = qi * tq

    def scores(kj):
        k0 = pl.multiple_of(kj * tk, tk)
        return _dot_nt(q2, kb_scr[pl.ds(k0, tk), :])

    def block(kj, masked):
        k0 = pl.multiple_of(kj * tk, tk)
        s = s_scr[...]
        if not masked:
            s_scr[...] = scores(kj + 1)
        if masked:
            r = lax.broadcasted_iota(jnp.int32, (2 * tq, 1), 0)
            q_chunk = lax.div(q0 + jnp.where(r >= tq, r - tq, r), CHUNK)
            k_chunk = lax.div(k0 + lax.broadcasted_iota(jnp.int32, (1, tk), 1), CHUNK)
            s = jnp.where(k_chunk <= q_chunk, s, -jnp.inf)
        tiles = [s[:, t * LANES:(t + 1) * LANES] for t in range(nlt)]
        m_cur = tiles[0]
        for t in range(1, nlt):
            m_cur = jnp.maximum(m_cur, tiles[t])
        m_old = m_scr[...]
        m_new = jnp.maximum(m_old, jnp.max(m_cur, axis=-1, keepdims=True))
        alpha = jnp.exp2(m_old - m_new)
        p = jnp.concatenate([jnp.exp2(tl - m_new).astype(BF16) for tl in tiles], axis=1)
        acc_scr[...] = (jnp.concatenate([alpha, alpha], axis=1) * acc_scr[...]
                        + _dot(p, vb_scr[pl.ds(k0, tk), :]))
        m_scr[...] = m_new

    def body(kj, carry):
        block(kj, False)
        return carry

    nfull = q0 // tk
    s_scr[...] = scores(0)
    lax.fori_loop(0, nfull, body, 0)
    block(nfull, True)
    lam = _lambda(lq1, lk1, lq2, lk2, lam_init)
    acc = acc_scr[...]
    _attn_finish(acc[0:tq, 0:LANES] / acc[0:tq, LANES:], acc[tq:, 0:LANES] / acc[tq:, LANES:],
                 lam, sg_ref, z_ref[...], o_ref, slice(None), lam_init)


def _attn_prompt(q3, k3, v3, z3, lq1, lk1, lq2, lk2, sg, lam_init, tq, tk):
    bsz, tlen, hd = q3.shape
    nh = hd // LANES
    assert tlen % tk == 0 and tk % tq == 0 and tq % CHUNK == 0
    qblk = pl.BlockSpec((None, tq, LANES), lambda bi, hi, qi: (bi, qi, hi))
    kblk = pl.BlockSpec((None, tlen, LANES), lambda bi, hi, qi: (bi, 0, hi))
    vec = lambda n: pl.BlockSpec((1, n), lambda bi, hi, qi: (0, 0))
    return pl.pallas_call(
        functools.partial(_attn_prompt_kernel, tq=tq, tk=tk, lam_init=lam_init),
        grid=(bsz, nh, tlen // tq),
        in_specs=[qblk, kblk, kblk, qblk, vec(HC_DIM), vec(HC_DIM), vec(HC_DIM), vec(HC_DIM), vec(LANES)],
        out_specs=qblk,
        out_shape=jax.ShapeDtypeStruct((bsz, tlen, hd), F32),
        scratch_shapes=[pltpu.VMEM((tlen, LANES), BF16), pltpu.VMEM((tlen, 2 * LANES), BF16),
                        pltpu.VMEM((2 * tq, LANES), F32), pltpu.VMEM((2 * tq, 2 * LANES), F32),
                        pltpu.VMEM((2 * tq, tk), F32)],
        compiler_params=_params(("parallel", "parallel", "arbitrary")),
        name="attn_prompt",
    )(q3, k3, v3, z3, lq1, lk1, lq2, lk2, sg)


def _attn_sample_kernel(q_ref, ck_ref, cv_ref, kn_ref, vn_ref, z_ref, lq1, lk1, lq2, lk2, sg_ref, o_ref,
                        m_scr, l_scr, acc_scr, *, lam_init, past):
    kb = pl.program_id(2)
    tlen = q_ref.shape[0]
    tkc, heads = ck_ref.shape[0], ck_ref.shape[1]
    per_head = 2 * tlen
    ncol = heads * per_head

    @pl.when(kb == 0)
    def _():
        m_scr[...] = jnp.full(m_scr.shape, -jnp.inf, F32)
        l_scr[...] = jnp.zeros(l_scr.shape, F32)
        acc_scr[...] = jnp.zeros(acc_scr.shape, F32)

    head_cols = [slice(h * LANES, (h + 1) * LANES) for h in range(heads)]
    scale = (HC_DIM ** -0.5) * LOG2E
    qw = jnp.concatenate([_map_queries(q_ref[:, cs], scale) for cs in head_cols], axis=0)
    col = lax.broadcasted_iota(jnp.int32, (1, ncol), 1)
    col_head = lax.div(col, per_head)
    q_chunk = lax.div(past + lax.rem(col, tlen), CHUNK)

    def update(k_ref, v_ref, row_head, row_pos):
        k, v = k_ref().astype(BF16), v_ref().astype(BF16)
        n, cyc = k.shape[0], row_head.shape[0]
        own = jnp.where(row_head == col_head, 0.0, -jnp.inf)
        s = (_dot_nt(k, qw).reshape(n // cyc, cyc, ncol) + own[None]).reshape(n, ncol)
        if row_pos is not None:
            s = jnp.where(lax.div(row_pos, CHUNK) <= q_chunk, s, -jnp.inf)
        m_old = m_scr[...]
        m_new = jnp.maximum(m_old, jnp.max(s, axis=0, keepdims=True))
        alpha = jnp.exp2(m_old - m_new)
        p = jnp.exp2(s - m_new)
        l_scr[...] = alpha * l_scr[...] + jnp.sum(p, axis=0, keepdims=True)
        acc_scr[...] = alpha * acc_scr[...] + _dot_tn(v, p.astype(BF16))
        m_scr[...] = m_new

    cached_k = lambda: ck_ref[...].reshape(tkc * heads, LANES)
    cached_v = lambda: cv_ref[...].reshape(tkc * heads, LANES)
    cached_head = lax.broadcasted_iota(jnp.int32, (heads, 1), 0)
    all_visible = (kb * tkc + tkc - 1) // CHUNK <= past // CHUNK

    @pl.when(all_visible)
    def _():
        update(cached_k, cached_v, cached_head, None)

    @pl.when(jnp.logical_not(all_visible))
    def _():
        row = lax.broadcasted_iota(jnp.int32, (tkc * heads, 1), 0)
        update(cached_k, cached_v, cached_head, kb * tkc + lax.div(row, heads))

    @pl.when(kb == pl.num_programs(2) - 1)
    def _():
        row_n = lax.broadcasted_iota(jnp.int32, (tlen * heads, 1), 0)
        update(lambda: jnp.concatenate([kn_ref[:, cs] for cs in head_cols], axis=0),
               lambda: jnp.concatenate([vn_ref[:, cs] for cs in head_cols], axis=0),
               lax.div(row_n, tlen), past + lax.rem(row_n, tlen))
        o = (acc_scr[...] / l_scr[...]).T
        lam = _lambda(lq1, lk1, lq2, lk2, lam_init)
        for h, cs in enumerate(head_cols):
            o_h = o[h * per_head:(h + 1) * per_head]
            _attn_finish(o_h[0:tlen], o_h[tlen:], lam, sg_ref, z_ref[:, cs], o_ref, cs, lam_init)


def _attn_sample(q3, ck5, cv5, layer, kn3, vn3, z3, lq1, lk1, lq2, lk2, sg, lam_init, tkc, heads):
    bsz, tlen, hd = q3.shape
    past, nh = ck5.shape[2], ck5.shape[3]
    assert past % tkc == 0 and nh % heads == 0
    ncol = heads * 2 * tlen
    new = pl.BlockSpec((None, tlen, heads * LANES), lambda bi, hi, ki: (bi, 0, hi))
    old = pl.BlockSpec((None, None, tkc, heads, LANES), lambda bi, hi, ki: (layer, bi, ki, hi, 0))
    vec = lambda n: pl.BlockSpec((1, n), lambda bi, hi, ki: (0, 0))
    stat = pltpu.VMEM((1, ncol), F32)
    return pl.pallas_call(
        functools.partial(_attn_sample_kernel, lam_init=lam_init, past=past),
        grid=(bsz, nh // heads, past // tkc),
        in_specs=[new, old, old, new, new, new, vec(HC_DIM), vec(HC_DIM), vec(HC_DIM), vec(HC_DIM), vec(LANES)],
        out_specs=new,
        out_shape=jax.ShapeDtypeStruct((bsz, tlen, hd), F32),
        scratch_shapes=[stat, stat, pltpu.VMEM((LANES, ncol), F32)],
        compiler_params=_params(("parallel", "parallel", "arbitrary")),
        name="attn_sample",
    )(q3, ck5, cv5, kn3, vn3, z3, lq1, lk1, lq2, lk2, sg)


def _row_tile(n, pref):
    while n % pref:
        pref //= 2
    return pref


def _even_layer(x3, conv_buf, s0, w_in, w_out, cw, cb, cg, cbb, lb_logits, hn_g, pg, pb, layer, alpha):
    bsz, tlen, d = x3.shape
    n = bsz * tlen
    x2 = x3.reshape(n, d)
    d_a = cw.shape[-1]
    h = _linear(x2, w_in, 0, w_in.shape[1], F32, _row_tile(n, 1024), 1024)
    h3 = h.reshape(bsz, tlen, -1)
    buf = jnp.pad(conv_buf, ((0, 0), (HIST - (CONV_W - 1), 0), (0, 0)))
    out_a, nbuf = _conv_branch(h3, buf, cw, cb, cg, cbb, _row_tile(tlen, 256))
    chunk = CHUNK if tlen % CHUNK == 0 else tlen
    tc = _row_tile(tlen, 512) if tlen % CHUNK == 0 else chunk
    d_b = s0.shape[1] * s0.shape[2]
    out_b, s_new = _hgrn_branch(h3, lb_logits, hn_g, s0, layer, 3 * d_a // d_b, tc, HGRN_HEADS_PER_STEP)
    xn = _out_ln(out_a.reshape(n, -1), 0, out_b.reshape(n, -1), 0, w_out, x2, pg, pb, alpha, _row_tile(n, 512))
    return xn.reshape(bsz, tlen, d), nbuf[:, HIST - (CONV_W - 1):], s_new


def _odd_layer(x3, caches, w_in, w_out, lq1, lk1, lq2, lk2, sg, pg, pb, lam_init, alpha):
    bsz, tlen, d = x3.shape
    n = bsz * tlen
    x2 = x3.reshape(n, d)
    d_c = w_in.shape[1] // 4
    tm = _row_tile(n, 1024)
    q, k, v, z = [_linear(x2, w_in, i * d_c, d_c, F32, tm, 1024).reshape(bsz, tlen, d_c) for i in range(4)]
    if caches is None:
        tk = _row_tile(tlen, 512)
        o = _attn_prompt(q, k, v, z, lq1, lk1, lq2, lk2, sg, lam_init, tk, tk)
    else:
        ck, cv, layer = caches
        o = _attn_sample(q, ck, cv, layer, k, v, z, lq1, lk1, lq2, lk2, sg, lam_init,
                         _row_tile(ck.shape[2], 512), SUBLANES)
    o2 = o.reshape(n, d_c)
    xn = _out_ln(o2, 0, o2, 1, w_out, x2, pg, pb, alpha, _row_tile(n, 512))
    nh = d_c // LANES
    return xn.reshape(bsz, tlen, d), k.reshape(bsz, tlen, nh, LANES), v.reshape(bsz, tlen, nh, LANES)


def kernel(x_prompt, x_sample, state_conv, state_hgrn, cache_k, cache_v, w_in_even, w_out_even, conv_w, conv_b,
           conv_ln_g, conv_ln_b, hgrn_lb_logits, hgrn_norm_g, w_in_odd, w_out_odd, lam_q1, lam_k1, lam_q2, lam_k2,
           subln_g, post_ln_g, post_ln_b):
    depth = post_ln_g.shape[0]
    alpha = (2.0 * depth) ** 0.25
    xp, xs = x_prompt, x_sample
    bp = xp.shape[0]
    row = lambda a: a.reshape(1, -1)
    outs = {name: [] for name in ("conv_p", "hgrn_p", "k_p", "v_p", "conv_s", "hgrn_s", "k_s", "v_s")}
    for l in range(depth):
        pg, pb = row(post_ln_g[l]), row(post_ln_b[l])
        if l % 2 == 0:
            e = l // 2
            prm = (w_in_even[e].astype(BF16), w_out_even[e].astype(BF16), conv_w[e], row(conv_b[e]),
                   row(conv_ln_g[e]), row(conv_ln_b[e]), hgrn_lb_logits, row(hgrn_norm_g[e]), pg, pb, e, alpha)
            buf0 = jnp.zeros((bp,) + state_conv.shape[2:], F32)
            s0 = jnp.zeros((bp,) + state_hgrn.shape[2:], F32)
            xp, cb_p, s_p = _even_layer(xp, buf0, s0, *prm)
            xs, cb_s, s_s = _even_layer(xs, state_conv[e], state_hgrn[e], *prm)
            outs["conv_p"].append(cb_p); outs["hgrn_p"].append(s_p)
            outs["conv_s"].append(cb_s); outs["hgrn_s"].append(s_s)
        else:
            o = l // 2
            lam_init = 0.8 - 0.6 * math.exp(-0.3 * l)
            prm = (w_in_odd[o].astype(BF16), w_out_odd[o].astype(BF16), row(lam_q1[o]), row(lam_k1[o]),
                   row(lam_q2[o]), row(lam_k2[o]), row(subln_g[o]), pg, pb, lam_init, alpha)
            xp, kn_p, vn_p = _odd_layer(xp, None, *prm)
            xs, kn_s, vn_s = _odd_layer(xs, (cache_k, cache_v, o), *prm)
            outs["k_p"].append(kn_p); outs["v_p"].append(vn_p)
            outs["k_s"].append(kn_s); outs["v_s"].append(vn_s)
    st = lambda name: jnp.stack(outs[name])
    return (xp, xs, st("conv_p"), st("hgrn_p"), st("k_p"), st("v_p"),
            st("conv_s"), st("hgrn_s"), st("k_s"), st("v_s"))
```

```python
import functools
import math

import jax
import jax.numpy as jnp
from jax import lax
from jax.experimental import pallas as pl
from jax.experimental.pallas import tpu as pltpu

F32 = jnp.float32
BF16 = jnp.bfloat16

CHUNK = 64
SUBLANES = 8
CONV_W = 31
HIST = 32
HB_DK = 128
HC_DIM = 64
LANES = 128
EPS = 1e-5
LOG2E = math.log2(math.e)
SAMPLE_ROW_PIECES = 4
HGRN_HEADS_PER_STEP = 8
VMEM_LIMIT = 48 * 1024 * 1024


def _params(sem):
    return pltpu.CompilerParams(dimension_semantics=sem, vmem_limit_bytes=VMEM_LIMIT)


def _dot(a, b):
    return jnp.dot(a, b, preferred_element_type=F32)


def _dot_nt(a, b):
    return lax.dot_general(a, b, (((1,), (1,)), ((), ())), preferred_element_type=F32)


def _dot_tn(a, b):
    return lax.dot_general(a, b, (((0,), (0,)), ((), ())), preferred_element_type=F32)


def _sigmoid(x):
    return 1.0 / (1.0 + jnp.exp(-x))


def _silu(x):
    return x * _sigmoid(x)


def _linear_kernel(x_ref, w_ref, *o_refs, blocks_per_out):
    j = pl.program_id(1)
    if len(o_refs) == 1:
        o_refs[0][...] = _dot(x_ref[...].astype(BF16), w_ref[...])
        return
    for k, o_ref in enumerate(o_refs):
        @pl.when((j >= k * blocks_per_out) & (j < (k + 1) * blocks_per_out))
        def _(o_ref=o_ref):
            o_ref[...] = _dot(x_ref[...].astype(BF16), w_ref[...])


def _linear(x, w, nout, tm, tn):
    n, k = x.shape
    width = w.shape[1] // nout
    assert n % tm == 0 and width % tn == 0 and width * nout == w.shape[1]
    bpo = width // tn

    def out_map(g):
        return lambda i, j: (i, jnp.clip(j - g * bpo, 0, bpo - 1))

    return pl.pallas_call(
        functools.partial(_linear_kernel, blocks_per_out=bpo),
        grid=(n // tm, nout * bpo),
        in_specs=[pl.BlockSpec((tm, k), lambda i, j: (i, 0)),
                  pl.BlockSpec((k, tn), lambda i, j: (0, j))],
        out_specs=[pl.BlockSpec((tm, tn), out_map(g)) for g in range(nout)],
        out_shape=[jax.ShapeDtypeStruct((n, width), F32) for _ in range(nout)],
        compiler_params=_params(("parallel", "arbitrary")),
        name="linear",
    )(x, w)


def _conv_kernel(a_ref, ga_ref, za_ref, buf_ref, w_ref, cb_ref, g_ref, b_ref, out_ref, nb_ref, scr, c_scr, sh_scr,
                 *, tt, rb):
    t = pl.program_id(1)
    nch = a_ref.shape[-1] // LANES

    @pl.when(t == 0)
    def _():
        scr[0:HIST, :] = buf_ref[...]

    @pl.when(t > 0)
    def _():
        scr[0:HIST, :] = scr[tt:tt + HIST, :]

    scr[HIST:HIST + tt, :] = a_ref[...] * _sigmoid(ga_ref[...])

    def row_block(r, carry):
        r0 = pl.multiple_of(r * rb, rb)
        for c in range(nch):
            cs = slice(c * LANES, (c + 1) * LANES)
            slab = scr[pl.ds(r0, rb + HIST), cs]
            for shift in range(SUBLANES):
                n = (rb + HIST - shift) // SUBLANES * SUBLANES
                sh_scr[shift, 0:n, :] = slab[shift:shift + n, :]
            acc = jnp.zeros((rb, LANES), F32)
            for j in range(CONV_W):
                shift = (j + 2) % SUBLANES
                off = j + 2 - shift
                acc = acc + w_ref[j:j + 1, cs] * sh_scr[shift, off:off + rb, :]
            c_scr[pl.ds(r0, rb), cs] = acc + cb_ref[:, cs]
        return carry

    lax.fori_loop(0, tt // rb, row_block, 0)

    cv = c_scr[...]
    mu = jnp.mean(cv, axis=-1, keepdims=True)
    xc = cv - mu
    var = jnp.mean(xc * xc, axis=-1, keepdims=True)
    y = xc * lax.rsqrt(var + EPS) * g_ref[...] + b_ref[...]
    out_ref[...] = _silu(y) * _silu(za_ref[...])

    @pl.when(t == pl.num_programs(1) - 1)
    def _():
        nb_ref[...] = scr[tt:tt + HIST, :]


def _conv_branch(h3, buf, cw, cb, g, b, tt):
    bsz, tlen, _ = h3.shape
    c = cw.shape[-1]
    assert tlen % tt == 0
    rb = min(64, tt)
    assert tt % rb == 0
    row = lambda col: pl.BlockSpec((None, tt, c), lambda bi, ti: (bi, ti, col))
    vec = lambda r: pl.BlockSpec((r, c), lambda bi, ti: (0, 0))
    hist = pl.BlockSpec((None, HIST, c), lambda bi, ti: (bi, 0, 0))
    return pl.pallas_call(
        functools.partial(_conv_kernel, tt=tt, rb=rb),
        grid=(bsz, tlen // tt),
        in_specs=[row(0), row(1), row(2), hist, vec(CONV_W), vec(1), vec(1), vec(1)],
        out_specs=[pl.BlockSpec((None, tt, c), lambda bi, ti: (bi, ti, 0)), hist],
        out_shape=[jax.ShapeDtypeStruct((bsz, tlen, c), F32), jax.ShapeDtypeStruct((bsz, HIST, c), F32)],
        scratch_shapes=[pltpu.VMEM((tt + HIST, c), F32), pltpu.VMEM((tt, c), F32),
                        pltpu.VMEM((SUBLANES, rb + HIST, LANES), F32)],
        compiler_params=_params(("parallel", "arbitrary")),
        name="conv_branch",
    )(h3, h3, h3, buf, cw, cb, g, b)


def _cumsum_rows(tri3, g):
    g0 = g.astype(BF16)
    r1 = g - g0.astype(F32)
    g1 = r1.astype(BF16)
    g2 = (r1 - g1.astype(F32)).astype(BF16)
    return _dot(tri3, jnp.concatenate([g0, g1, g2], axis=0))


def _pair_reference(x, h):
    nrows, width = x.shape
    if h >= SUBLANES:
        parts = [jnp.broadcast_to(x[p + h - 1:p + h], (2 * h, width)) for p in range(0, nrows, 2 * h)]
        return parts[0] if len(parts) == 1 else jnp.concatenate(parts, axis=0)
    x3 = x.reshape(nrows // SUBLANES, SUBLANES, width)
    sub = lax.broadcasted_iota(jnp.int32, (1, SUBLANES, 1), 1)
    ref = None
    for p in range(SUBLANES - 2 * h, -1, -2 * h):
        row = x3[:, p + h - 1:p + h, :]
        ref = row if ref is None else jnp.where(sub < p + 2 * h, row, ref)
    return jnp.broadcast_to(ref, x3.shape).reshape(nrows, width)


def _hgrn_kernel(q_ref, f_ref, i_ref, zb_ref, lbl_ref, hg_ref, s0_ref, o_ref, s_ref, st_scr,
                 *, layer, chunk, nchunks, heads):
    t = pl.program_id(2)
    dk = HB_DK

    @pl.when(t == 0)
    def _():
        for h in range(heads):
            st_scr[h] = s0_ref[h].T

    lg = lbl_ref[...]
    ex = jnp.exp(lg - jnp.max(lg, axis=0, keepdims=True))
    sm = ex / jnp.sum(ex, axis=0, keepdims=True)
    lb = jnp.zeros_like(sm[0:1])
    for r in range(1, layer + 1):
        lb = lb + sm[r:r + 1]
    log_lb = jnp.log(lb)
    log1m_lb = jnp.log1p(-lb)

    rows = lax.broadcasted_iota(jnp.int32, (chunk, chunk), 0)
    cols = lax.broadcasted_iota(jnp.int32, (chunk, chunk), 1)
    tri = jnp.where(rows >= cols, 1.0, 0.0).astype(BF16)
    tri3 = jnp.concatenate([tri, tri, tri], axis=1)
    rows2 = lax.broadcasted_iota(jnp.int32, (chunk, 2 * chunk), 0)
    cols2 = lax.rem(lax.broadcasted_iota(jnp.int32, (chunk, 2 * chunk), 1), chunk)
    xor2 = jnp.bitwise_xor(rows2, cols2)
    levels = [0]
    h = 1
    while h < chunk:
        levels.append(h)
        h *= 2
    heads_cols = [slice(hd * dk, (hd + 1) * dk) for hd in range(heads)]
    zero_blk = jnp.zeros((chunk, dk), BF16)

    def block_diag(x, g):
        a, b = x[:, heads_cols[2 * g]], x[:, heads_cols[2 * g + 1]]
        return jnp.concatenate([jnp.concatenate([a, zero_blk], axis=1), jnp.concatenate([zero_blk, b], axis=1)],
                               axis=0)

    def one_chunk(ci, carry):
        r0 = pl.multiple_of(ci * chunk, chunk)
        rs = pl.ds(r0, chunk)
        fr = f_ref[rs, :]
        lsig = jnp.minimum(fr, 0.0) - jnp.log(1.0 + jnp.exp(-jnp.abs(fr)))
        c2 = log1m_lb + lsig
        logf = jnp.maximum(log_lb, c2) + jnp.log(1.0 + jnp.exp(-jnp.abs(log_lb - c2)))
        kk = (1.0 - lb) * _sigmoid(-fr)
        qq = _silu(q_ref[rs, :])
        vv = i_ref[rs, :]
        vb = vv.astype(BF16)
        bl = _cumsum_rows(tri3, logf) * LOG2E
        qe = (qq * jnp.exp2(bl)).astype(BF16)
        o = jnp.concatenate([_dot_nt(qe[:, cs], st_scr[hd].astype(BF16)) for hd, cs in enumerate(heads_cols)],
                            axis=1)
        att = [jnp.zeros((chunk, 2 * chunk), F32) for _ in range(heads // 2)]
        for h in levels:
            if h == 0:
                qf, kf, keep = qq.astype(BF16), kk.astype(BF16), rows2 == cols2
            else:
                fac = jnp.exp2(-jnp.abs(bl - _pair_reference(bl, h)))
                qf, kf = (qq * fac).astype(BF16), (kk * fac).astype(BF16)
                keep = (xor2 >= h) & (xor2 < 2 * h) & (rows2 > cols2)
            for g in range(heads // 2):
                a_h = _dot_nt(qf[:, 2 * g * dk:(2 * g + 2) * dk], block_diag(kf, g))
                att[g] = jnp.where(keep, a_h, att[g])
        o = o + jnp.concatenate([_dot(att[g].astype(BF16), block_diag(vb, g)) for g in range(heads // 2)], axis=1)
        bl_last = bl[chunk - 1:chunk]
        kd = (kk * jnp.exp2(bl_last - bl)).astype(BF16)
        decay = jnp.exp2(bl_last)
        for hd, cs in enumerate(heads_cols):
            st_scr[hd] = st_scr[hd] * decay[:, cs] + _dot_tn(vb[:, cs], kd[:, cs])
        inv = jnp.concatenate(
            [jnp.broadcast_to(lax.rsqrt(jnp.mean(o[:, cs] * o[:, cs], axis=-1, keepdims=True) + EPS), (chunk, dk))
             for cs in heads_cols], axis=1)
        o_ref[rs, :] = o * inv * jnp.concatenate([hg_ref[...]] * heads, axis=1) * _silu(zb_ref[rs, :])
        return carry

    lax.fori_loop(0, nchunks, one_chunk, 0)

    @pl.when(t == pl.num_programs(2) - 1)
    def _():
        for h in range(heads):
            s_ref[h] = st_scr[h].T


def _hgrn_branch(h3, lb_logits, hn_g, s0, layer, col_q, tc, heads):
    bsz, tlen, _ = h3.shape
    nh, dk, dv = s0.shape[1:]
    d_b = nh * dk
    chunk = CHUNK if tlen % CHUNK == 0 else tlen
    assert chunk % SUBLANES == 0 and chunk & (chunk - 1) == 0
    assert tc % chunk == 0 and tlen % tc == 0 and nh % heads == 0
    w = heads * dk
    gpb = d_b // w
    col = lambda k: pl.BlockSpec((None, tc, w), lambda bi, hi, ti: (bi, ti, (col_q + k) * gpb + hi))
    st = pl.BlockSpec((None, heads, dk, dv), lambda bi, hi, ti: (bi, hi, 0, 0))
    ne = lb_logits.shape[0]
    return pl.pallas_call(
        functools.partial(_hgrn_kernel, layer=layer, chunk=chunk, nchunks=tc // chunk, heads=heads),
        grid=(bsz, nh // heads, tlen // tc),
        in_specs=[col(0), col(1), col(2), col(3),
                  pl.BlockSpec((ne, w), lambda bi, hi, ti: (0, hi)),
                  pl.BlockSpec((1, dv), lambda bi, hi, ti: (0, 0)),
                  st],
        out_specs=[pl.BlockSpec((None, tc, w), lambda bi, hi, ti: (bi, ti, hi)), st],
        out_shape=[jax.ShapeDtypeStruct((bsz, tlen, d_b), F32), jax.ShapeDtypeStruct(s0.shape, F32)],
        scratch_shapes=[pltpu.VMEM((heads, dv, dk), F32)],
        compiler_params=_params(("parallel", "parallel", "arbitrary")),
        name="hgrn_branch",
    )(h3, h3, h3, h3, lb_logits, hn_g, s0)


def _out_ln_kernel(a_ref, b_ref, w_ref, x_ref, g_ref, bb_ref, o_ref, *, alpha):
    ka = a_ref.shape[-1]
    y = _dot(a_ref[...].astype(BF16), w_ref[0:ka, :]) + _dot(b_ref[...].astype(BF16), w_ref[ka:, :])
    r = alpha * x_ref[...] + y
    mu = jnp.mean(r, axis=-1, keepdims=True)
    rc = r - mu
    var = jnp.mean(rc * rc, axis=-1, keepdims=True)
    o_ref[...] = rc * lax.rsqrt(var + EPS) * g_ref[...] + bb_ref[...]


def _out_ln(a, a_col, b, b_col, w, x, g, bb, alpha, tm):
    n, d = x.shape
    kh = w.shape[0] // 2
    assert n % tm == 0
    return pl.pallas_call(
        functools.partial(_out_ln_kernel, alpha=alpha),
        grid=(n // tm,),
        in_specs=[pl.BlockSpec((tm, kh), lambda i: (i, a_col)),
                  pl.BlockSpec((tm, kh), lambda i: (i, b_col)),
                  pl.BlockSpec(w.shape, lambda i: (0, 0)),
                  pl.BlockSpec((tm, d), lambda i: (i, 0)),
                  pl.BlockSpec((1, d), lambda i: (0, 0)),
                  pl.BlockSpec((1, d), lambda i: (0, 0))],
        out_specs=pl.BlockSpec((tm, d), lambda i: (i, 0)),
        out_shape=jax.ShapeDtypeStruct((n, d), F32),
        compiler_params=_params(("parallel",)),
        name="out_ln",
    )(a, b, w, x, g, bb)


def _lambda(lq1, lk1, lq2, lk2, lam_init):
    e1 = jnp.exp(jnp.sum(lq1[...] * lk1[...], axis=-1, keepdims=True))
    e2 = jnp.exp(jnp.sum(lq2[...] * lk2[...], axis=-1, keepdims=True))
    return e1 - e2 + lam_init


def _map_queries(q, scale):
    q = q * scale
    lane = lax.broadcasted_iota(jnp.int32, q.shape, 1)
    return jnp.concatenate([jnp.where(lane < HC_DIM, q, 0.0), jnp.where(lane >= HC_DIM, q, 0.0)],
                           axis=0).astype(BF16)


def _attn_finish(o1, o2, lam, sg_ref, z, o_ref, cols, lam_init):
    o = o1 - lam * o2
    o = o * lax.rsqrt(jnp.mean(o * o, axis=-1, keepdims=True) + EPS) * sg_ref[...] * (1.0 - lam_init)
    o_ref[:, cols] = o * _silu(z)


def _attn_prompt_kernel(q_ref, k_ref, v_ref, z_ref, lq1, lk1, lq2, lk2, sg_ref, o_ref, kb_scr, vb_scr, m_scr, acc_scr, s_scr, bias_scr,
                        *, tq, tk, lam_init):
    qi = pl.program_id(2)
    tlen = k_ref.shape[0]
    nlt = tk // LANES

    @pl.when(qi == 0)
    def _():
        kb_scr[...] = k_ref[...].astype(BF16)
        vb_scr[:, 0:LANES] = v_ref[...].astype(BF16)
        vb_scr[:, LANES:] = jnp.ones((tlen, LANES), BF16)
        q_chunk = lax.div(lax.broadcasted_iota(jnp.int32, (tq, 1), 0), CHUNK)
        k_chunk = lax.div(lax.broadcasted_iota(jnp.int32, (1, tk), 1), CHUNK)
        bias_scr[...] = jnp.where(k_chunk <= q_chunk, 0.0, -jnp.inf)

    q2 = _map_queries(q_ref[...], (HC_DIM ** -0.5) * LOG2E)
    m_scr[...] = jnp.full(m_scr.shape, -jnp.inf, F32)
    acc_scr[...] = jnp.zeros(acc_scr.shape, F32)

    def scores(kj):
        k0 = pl.multiple_of(kj * tk, tk)
        return _dot_nt(q2, kb_scr[pl.ds(k0, tk), :])

    def block(kj, masked, prefetch):
        k0 = pl.multiple_of(kj * tk, tk)
        s = s_scr[...]
        if prefetch is not None:
            s_scr[...] = scores(prefetch)
        if masked:
            bias = bias_scr[...]
            s = s + jnp.concatenate([bias, bias], axis=0)
        tiles = [s[:, t * LANES:(t + 1) * LANES] for t in range(nlt)]
        m_cur = tiles[0]
        for t in range(1, nlt):
            m_cur = jnp.maximum(m_cur, tiles[t])
        m_old = m_scr[...]
        m_new = jnp.maximum(m_old, jnp.max(m_cur, axis=-1, keepdims=True))
        alpha = jnp.exp2(m_old - m_new)
        p = jnp.concatenate([jnp.exp2(tl - m_new).astype(BF16) for tl in tiles], axis=1)
        acc_scr[...] = (jnp.concatenate([alpha, alpha], axis=1) * acc_scr[...]
                        + _dot(p, vb_scr[pl.ds(k0, tk), :]))
        m_scr[...] = m_new

    def body(kj, carry):
        block(kj, False, kj + 1)
        return carry

    s_scr[...] = scores(qi)
    block(qi, True, 0)
    lax.fori_loop(0, qi - 1, body, 0)

    @pl.when(qi > 0)
    def _():
        block(qi - 1, False, None)

    lam = _lambda(lq1, lk1, lq2, lk2, lam_init)
    acc = acc_scr[...]
    _attn_finish(acc[0:tq, 0:LANES] / acc[0:tq, LANES:], acc[tq:, 0:LANES] / acc[tq:, LANES:],
                 lam, sg_ref, z_ref[...], o_ref, slice(None), lam_init)


def _attn_prompt(q3, k3, v3, z3, lq1, lk1, lq2, lk2, sg, lam_init, tq, tk):
    bsz, tlen, hd = q3.shape
    nh = hd // LANES
    assert tlen % tk == 0 and tk == tq and tq % CHUNK == 0
    qblk = pl.BlockSpec((None, tq, LANES), lambda bi, hi, qi: (bi, qi, hi))
    kblk = pl.BlockSpec((None, tlen, LANES), lambda bi, hi, qi: (bi, 0, hi))
    vec = lambda n: pl.BlockSpec((1, n), lambda bi, hi, qi: (0, 0))
    return pl.pallas_call(
        functools.partial(_attn_prompt_kernel, tq=tq, tk=tk, lam_init=lam_init),
        grid=(bsz, nh, tlen // tq),
        in_specs=[qblk, kblk, kblk, qblk, vec(HC_DIM), vec(HC_DIM), vec(HC_DIM), vec(HC_DIM), vec(LANES)],
        out_specs=qblk,
        out_shape=jax.ShapeDtypeStruct((bsz, tlen, hd), F32),
        scratch_shapes=[pltpu.VMEM((tlen, LANES), BF16), pltpu.VMEM((tlen, 2 * LANES), BF16),
                        pltpu.VMEM((2 * tq, LANES), F32), pltpu.VMEM((2 * tq, 2 * LANES), F32),
                        pltpu.VMEM((2 * tq, tk), F32), pltpu.VMEM((tq, tk), F32)],
        compiler_params=_params(("parallel", "parallel", "arbitrary")),
        name="attn_prompt",
    )(q3, k3, v3, z3, lq1, lk1, lq2, lk2, sg)


def _attn_sample_kernel(q_ref, ck_ref, cv_ref, kn_ref, vn_ref, z_ref, lq1, lk1, lq2, lk2, sg_ref, o_ref,
                        m_scr, l_scr, acc_scr, *, lam_init, past):
    kb = pl.program_id(2)
    tlen = q_ref.shape[0]
    tkc, heads = ck_ref.shape[0], ck_ref.shape[1]
    per_head = 2 * tlen
    ncol = heads * per_head

    @pl.when(kb == 0)
    def _():
        m_scr[...] = jnp.full(m_scr.shape, -jnp.inf, F32)
        l_scr[...] = jnp.zeros(l_scr.shape, F32)
        acc_scr[...] = jnp.zeros(acc_scr.shape, F32)

    head_cols = [slice(h * LANES, (h + 1) * LANES) for h in range(heads)]
    scale = (HC_DIM ** -0.5) * LOG2E
    qw = jnp.concatenate([_map_queries(q_ref[:, cs], scale) for cs in head_cols], axis=0)
    col = lax.broadcasted_iota(jnp.int32, (1, ncol), 1)
    col_head = lax.div(col, per_head)
    q_chunk = lax.div(past + lax.rem(col, tlen), CHUNK)

    def update(k_ref, v_ref, row_head, row_pos):
        k, v = k_ref().astype(BF16), v_ref().astype(BF16)
        n, cyc = k.shape[0], row_head.shape[0]
        own = jnp.where(row_head == col_head, 0.0, -jnp.inf)
        pieces = SAMPLE_ROW_PIECES if n % (SAMPLE_ROW_PIECES * cyc) == 0 else 1
        step = n // pieces
        scores = lambda i: _dot_nt(k[i * step:(i + 1) * step], qw)
        s_next = scores(0)
        for i in range(pieces):
            s = s_next
            if i + 1 < pieces:
                s_next = scores(i + 1)
            s = (s.reshape(step // cyc, cyc, ncol) + own[None]).reshape(step, ncol)
            if row_pos is not None:
                s = jnp.where(lax.div(row_pos[i * step:(i + 1) * step], CHUNK) <= q_chunk, s, -jnp.inf)
            m_old = m_scr[...]
            m_new = jnp.maximum(m_old, jnp.max(s, axis=0, keepdims=True))
            alpha = jnp.exp2(m_old - m_new)
            p = jnp.exp2(s - m_new)
            l_scr[...] = alpha * l_scr[...] + jnp.sum(p, axis=0, keepdims=True)
            acc_scr[...] = alpha * acc_scr[...] + _dot_tn(v[i * step:(i + 1) * step], p.astype(BF16))
            m_scr[...] = m_new

    cached_k = lambda: ck_ref[...].reshape(tkc * heads, LANES)
    cached_v = lambda: cv_ref[...].reshape(tkc * heads, LANES)
    cached_head = lax.broadcasted_iota(jnp.int32, (heads, 1), 0)
    all_visible = (kb * tkc + tkc - 1) // CHUNK <= past // CHUNK

    @pl.when(all_visible)
    def _():
        update(cached_k, cached_v, cached_head, None)

    @pl.when(jnp.logical_not(all_visible))
    def _():
        row = lax.broadcasted_iota(jnp.int32, (tkc * heads, 1), 0)
        update(cached_k, cached_v, cached_head, kb * tkc + lax.div(row, heads))

    @pl.when(kb == pl.num_programs(2) - 1)
    def _():
        row_n = lax.broadcasted_iota(jnp.int32, (tlen * heads, 1), 0)
        update(lambda: jnp.concatenate([kn_ref[:, cs] for cs in head_cols], axis=0),
               lambda: jnp.concatenate([vn_ref[:, cs] for cs in head_cols], axis=0),
               lax.div(row_n, tlen), past + lax.rem(row_n, tlen))
        o = (acc_scr[...] / l_scr[...]).T
        lam = _lambda(lq1, lk1, lq2, lk2, lam_init)
        for h, cs in enumerate(head_cols):
            o_h = o[h * per_head:(h + 1) * per_head]
            _attn_finish(o_h[0:tlen], o_h[tlen:], lam, sg_ref, z_ref[:, cs], o_ref, cs, lam_init)


def _attn_sample(q3, ck5, cv5, layer, kn3, vn3, z3, lq1, lk1, lq2, lk2, sg, lam_init, tkc, heads):
    bsz, tlen, hd = q3.shape
    past, nh = ck5.shape[2], ck5.shape[3]
    assert past % tkc == 0 and nh % heads == 0
    ncol = heads * 2 * tlen
    new = pl.BlockSpec((None, tlen, heads * LANES), lambda bi, hi, ki: (bi, 0, hi))
    old = pl.BlockSpec((None, None, tkc, heads, LANES), lambda bi, hi, ki: (layer, bi, ki, hi, 0))
    vec = lambda n: pl.BlockSpec((1, n), lambda bi, hi, ki: (0, 0))
    stat = pltpu.VMEM((1, ncol), F32)
    return pl.pallas_call(
        functools.partial(_attn_sample_kernel, lam_init=lam_init, past=past),
        grid=(bsz, nh // heads, past // tkc),
        in_specs=[new, old, old, new, new, new, vec(HC_DIM), vec(HC_DIM), vec(HC_DIM), vec(HC_DIM), vec(LANES)],
        out_specs=new,
        out_shape=jax.ShapeDtypeStruct((bsz, tlen, hd), F32),
        scratch_shapes=[stat, stat, pltpu.VMEM((LANES, ncol), F32)],
        compiler_params=_params(("parallel", "parallel", "arbitrary")),
        name="attn_sample",
    )(q3, ck5, cv5, kn3, vn3, z3, lq1, lk1, lq2, lk2, sg)


def _row_tile(n, pref):
    while n % pref:
        pref //= 2
    return pref


def _even_layer(x3, conv_buf, s0, w_in, w_out, cw, cb, cg, cbb, lb_logits, hn_g, pg, pb, layer, alpha):
    bsz, tlen, d = x3.shape
    n = bsz * tlen
    x2 = x3.reshape(n, d)
    d_a = cw.shape[-1]
    (h,) = _linear(x2, w_in, 1, _row_tile(n, 1024), 1024)
    h3 = h.reshape(bsz, tlen, -1)
    buf = jnp.pad(conv_buf, ((0, 0), (HIST - (CONV_W - 1), 0), (0, 0)))
    out_a, nbuf = _conv_branch(h3, buf, cw, cb, cg, cbb, _row_tile(tlen, 256))
    chunk = CHUNK if tlen % CHUNK == 0 else tlen
    tc = _row_tile(tlen, 512) if tlen % CHUNK == 0 else chunk
    d_b = s0.shape[1] * s0.shape[2]
    out_b, s_new = _hgrn_branch(h3, lb_logits, hn_g, s0, layer, 3 * d_a // d_b, tc, HGRN_HEADS_PER_STEP)
    xn = _out_ln(out_a.reshape(n, -1), 0, out_b.reshape(n, -1), 0, w_out, x2, pg, pb, alpha, _row_tile(n, 512))
    return xn.reshape(bsz, tlen, d), nbuf[:, HIST - (CONV_W - 1):], s_new


def _odd_layer(x3, caches, w_in, w_out, lq1, lk1, lq2, lk2, sg, pg, pb, lam_init, alpha):
    bsz, tlen, d = x3.shape
    n = bsz * tlen
    x2 = x3.reshape(n, d)
    d_c = w_in.shape[1] // 4
    tm = _row_tile(n, 1024)
    q, k, v, z = [y.reshape(bsz, tlen, d_c) for y in _linear(x2, w_in, 4, tm, 1024)]
    if caches is None:
        tk = _row_tile(tlen, 512)
        o = _attn_prompt(q, k, v, z, lq1, lk1, lq2, lk2, sg, lam_init, tk, tk)
    else:
        ck, cv, layer = caches
        o = _attn_sample(q, ck, cv, layer, k, v, z, lq1, lk1, lq2, lk2, sg, lam_init,
                         _row_tile(ck.shape[2], 512), SUBLANES)
    o2 = o.reshape(n, d_c)
    xn = _out_ln(o2, 0, o2, 1, w_out, x2, pg, pb, alpha, _row_tile(n, 512))
    nh = d_c // LANES
    return xn.reshape(bsz, tlen, d), k.reshape(bsz, tlen, nh, LANES), v.reshape(bsz, tlen, nh, LANES)


def kernel(x_prompt, x_sample, state_conv, state_hgrn, cache_k, cache_v, w_in_even, w_out_even, conv_w, conv_b,
           conv_ln_g, conv_ln_b, hgrn_lb_logits, hgrn_norm_g, w_in_odd, w_out_odd, lam_q1, lam_k1, lam_q2, lam_k2,
           subln_g, post_ln_g, post_ln_b):
    depth = post_ln_g.shape[0]
    alpha = (2.0 * depth) ** 0.25
    xp, xs = x_prompt, x_sample
    bp = xp.shape[0]
    row = lambda a: a.reshape(1, -1)
    outs = {name: [] for name in ("conv_p", "hgrn_p", "k_p", "v_p", "conv_s", "hgrn_s", "k_s", "v_s")}
    for l in range(depth):
        pg, pb = row(post_ln_g[l]), row(post_ln_b[l])
        if l % 2 == 0:
            e = l // 2
            prm = (w_in_even[e].astype(BF16), w_out_even[e].astype(BF16), conv_w[e], row(conv_b[e]),
                   row(conv_ln_g[e]), row(conv_ln_b[e]), hgrn_lb_logits, row(hgrn_norm_g[e]), pg, pb, e, alpha)
            buf0 = jnp.zeros((bp,) + state_conv.shape[2:], F32)
            s0 = jnp.zeros((bp,) + state_hgrn.shape[2:], F32)
            xp, cb_p, s_p = _even_layer(xp, buf0, s0, *prm)
            xs, cb_s, s_s = _even_layer(xs, state_conv[e], state_hgrn[e], *prm)
            outs["conv_p"].append(cb_p); outs["hgrn_p"].append(s_p)
            outs["conv_s"].append(cb_s); outs["hgrn_s"].append(s_s)
        else:
            o = l // 2
            lam_init = 0.8 - 0.6 * math.exp(-0.3 * l)
            prm = (w_in_odd[o].astype(BF16), w_out_odd[o].astype(BF16), row(lam_q1[o]), row(lam_k1[o]),
                   row(lam_q2[o]), row(lam_k2[o]), row(subln_g[o]), pg, pb, lam_init, alpha)
            xp, kn_p, vn_p = _odd_layer(xp, None, *prm)
            xs, kn_s, vn_s = _odd_layer(xs, (cache_k, cache_v, o), *prm)
            outs["k_p"].append(kn_p); outs["v_p"].append(vn_p)
            outs["k_s"].append(kn_s); outs["v_s"].append(vn_s)
    st = lambda name: jnp.stack(outs[name])
    return (xp, xs, st("conv_p"), st("hgrn_p"), st("k_p"), st("v_p"),
            st("conv_s"), st("hgrn_s"), st("k_s"), st("v_s"))
```

```python
import functools
import math

import jax
import jax.numpy as jnp
from jax import lax
from jax.experimental import pallas as pl
from jax.experimental.pallas import tpu as pltpu

F32 = jnp.float32
BF16 = jnp.bfloat16

CHUNK = 64
SUBLANES = 8
CONV_W = 31
HIST = 32
HB_DK = 128
HC_DIM = 64
LANES = 128
EPS = 1e-5
LOG2E = math.log2(math.e)
SAMPLE_ROW_PIECES = 4
HGRN_HEADS_PER_STEP = 8
VMEM_LIMIT = 48 * 1024 * 1024


def _params(sem):
    return pltpu.CompilerParams(dimension_semantics=sem, vmem_limit_bytes=VMEM_LIMIT)


def _dot(a, b):
    return jnp.dot(a, b, preferred_element_type=F32)


def _dot_nt(a, b):
    return lax.dot_general(a, b, (((1,), (1,)), ((), ())), preferred_element_type=F32)


def _dot_tn(a, b):
    return lax.dot_general(a, b, (((0,), (0,)), ((), ())), preferred_element_type=F32)


def _sigmoid(x):
    return 1.0 / (1.0 + jnp.exp(-x))


def _silu(x):
    return x * _sigmoid(x)


def _linear_kernel(x_ref, w_ref, *o_refs, blocks_per_out):
    j = pl.program_id(1)
    if len(o_refs) == 1:
        o_refs[0][...] = _dot(x_ref[...].astype(BF16), w_ref[...])
        return
    for k, o_ref in enumerate(o_refs):
        @pl.when((j >= k * blocks_per_out) & (j < (k + 1) * blocks_per_out))
        def _(o_ref=o_ref):
            o_ref[...] = _dot(x_ref[...].astype(BF16), w_ref[...])


def _linear(x, w, nout, tm, tn):
    n, k = x.shape
    width = w.shape[1] // nout
    assert n % tm == 0 and width % tn == 0 and width * nout == w.shape[1]
    bpo = width // tn

    def out_map(g):
        return lambda i, j: (i, jnp.clip(j - g * bpo, 0, bpo - 1))

    return pl.pallas_call(
        functools.partial(_linear_kernel, blocks_per_out=bpo),
        grid=(n // tm, nout * bpo),
        in_specs=[pl.BlockSpec((tm, k), lambda i, j: (i, 0)),
                  pl.BlockSpec((k, tn), lambda i, j: (0, j))],
        out_specs=[pl.BlockSpec((tm, tn), out_map(g)) for g in range(nout)],
        out_shape=[jax.ShapeDtypeStruct((n, width), F32) for _ in range(nout)],
        compiler_params=_params(("parallel", "arbitrary")),
        name="linear",
    )(x, w)


def _conv_kernel(a_ref, ga_ref, za_ref, buf_ref, w_ref, cb_ref, g_ref, b_ref, out_ref, nb_ref, scr, c_scr, sh_scr,
                 *, tt, rb):
    t = pl.program_id(1)
    nch = a_ref.shape[-1] // LANES

    @pl.when(t == 0)
    def _():
        scr[0:HIST, :] = buf_ref[...]

    @pl.when(t > 0)
    def _():
        scr[0:HIST, :] = scr[tt:tt + HIST, :]

    scr[HIST:HIST + tt, :] = a_ref[...] * _sigmoid(ga_ref[...])

    def row_block(r, carry):
        r0 = pl.multiple_of(r * rb, rb)
        for c in range(nch):
            cs = slice(c * LANES, (c + 1) * LANES)
            slab = scr[pl.ds(r0, rb + HIST), cs]
            for shift in range(SUBLANES):
                n = (rb + HIST - shift) // SUBLANES * SUBLANES
                sh_scr[shift, 0:n, :] = slab[shift:shift + n, :]
            acc = jnp.zeros((rb, LANES), F32)
            for j in range(CONV_W):
                shift = (j + 2) % SUBLANES
                off = j + 2 - shift
                acc = acc + w_ref[j:j + 1, cs] * sh_scr[shift, off:off + rb, :]
            c_scr[pl.ds(r0, rb), cs] = acc + cb_ref[:, cs]
        return carry

    lax.fori_loop(0, tt // rb, row_block, 0)

    cv = c_scr[...]
    mu = jnp.mean(cv, axis=-1, keepdims=True)
    xc = cv - mu
    var = jnp.mean(xc * xc, axis=-1, keepdims=True)
    y = xc * lax.rsqrt(var + EPS) * g_ref[...] + b_ref[...]
    out_ref[...] = _silu(y) * _silu(za_ref[...])

    @pl.when(t == pl.num_programs(1) - 1)
    def _():
        nb_ref[...] = scr[tt:tt + HIST, :]


def _conv_branch(h3, buf, cw, cb, g, b, tt):
    bsz, tlen, _ = h3.shape
    c = cw.shape[-1]
    assert tlen % tt == 0
    rb = min(64, tt)
    assert tt % rb == 0
    row = lambda col: pl.BlockSpec((None, tt, c), lambda bi, ti: (bi, ti, col))
    vec = lambda r: pl.BlockSpec((r, c), lambda bi, ti: (0, 0))
    hist = pl.BlockSpec((None, HIST, c), lambda bi, ti: (bi, 0, 0))
    return pl.pallas_call(
        functools.partial(_conv_kernel, tt=tt, rb=rb),
        grid=(bsz, tlen // tt),
        in_specs=[row(0), row(1), row(2), hist, vec(CONV_W), vec(1), vec(1), vec(1)],
        out_specs=[pl.BlockSpec((None, tt, c), lambda bi, ti: (bi, ti, 0)), hist],
        out_shape=[jax.ShapeDtypeStruct((bsz, tlen, c), F32), jax.ShapeDtypeStruct((bsz, HIST, c), F32)],
        scratch_shapes=[pltpu.VMEM((tt + HIST, c), F32), pltpu.VMEM((tt, c), F32),
                        pltpu.VMEM((SUBLANES, rb + HIST, LANES), F32)],
        compiler_params=_params(("parallel", "arbitrary")),
        name="conv_branch",
    )(h3, h3, h3, buf, cw, cb, g, b)


def _cumsum_rows(tri3, g):
    g0 = g.astype(BF16)
    r1 = g - g0.astype(F32)
    g1 = r1.astype(BF16)
    g2 = (r1 - g1.astype(F32)).astype(BF16)
    return _dot(tri3, jnp.concatenate([g0, g1, g2], axis=0))


def _pair_reference(x, h):
    nrows, width = x.shape
    if h >= SUBLANES:
        parts = [jnp.broadcast_to(x[p + h - 1:p + h], (2 * h, width)) for p in range(0, nrows, 2 * h)]
        return parts[0] if len(parts) == 1 else jnp.concatenate(parts, axis=0)
    x3 = x.reshape(nrows // SUBLANES, SUBLANES, width)
    sub = lax.broadcasted_iota(jnp.int32, (1, SUBLANES, 1), 1)
    ref = None
    for p in range(SUBLANES - 2 * h, -1, -2 * h):
        row = x3[:, p + h - 1:p + h, :]
        ref = row if ref is None else jnp.where(sub < p + 2 * h, row, ref)
    return jnp.broadcast_to(ref, x3.shape).reshape(nrows, width)


def _hgrn_kernel(q_ref, f_ref, i_ref, zb_ref, lbl_ref, hg_ref, s0_ref, o_ref, s_ref, st_scr,
                 *, layer, chunk, nchunks, heads):
    t = pl.program_id(2)
    dk = HB_DK

    @pl.when(t == 0)
    def _():
        for h in range(heads):
            st_scr[h] = s0_ref[h].T

    lg = lbl_ref[...]
    ex = jnp.exp(lg - jnp.max(lg, axis=0, keepdims=True))
    sm = ex / jnp.sum(ex, axis=0, keepdims=True)
    lb = jnp.zeros_like(sm[0:1])
    for r in range(1, layer + 1):
        lb = lb + sm[r:r + 1]
    log_lb = jnp.log(lb)
    log1m_lb = jnp.log1p(-lb)

    rows = lax.broadcasted_iota(jnp.int32, (chunk, chunk), 0)
    cols = lax.broadcasted_iota(jnp.int32, (chunk, chunk), 1)
    tri = jnp.where(rows >= cols, 1.0, 0.0).astype(BF16)
    tri3 = jnp.concatenate([tri, tri, tri], axis=1)
    rows2 = lax.broadcasted_iota(jnp.int32, (chunk, 2 * chunk), 0)
    cols2 = lax.rem(lax.broadcasted_iota(jnp.int32, (chunk, 2 * chunk), 1), chunk)
    xor2 = jnp.bitwise_xor(rows2, cols2)
    levels = [0]
    h = 1
    while h < chunk:
        levels.append(h)
        h *= 2
    heads_cols = [slice(hd * dk, (hd + 1) * dk) for hd in range(heads)]
    zero_blk = jnp.zeros((chunk, dk), BF16)

    def block_diag(x, g):
        a, b = x[:, heads_cols[2 * g]], x[:, heads_cols[2 * g + 1]]
        return jnp.concatenate([jnp.concatenate([a, zero_blk], axis=1), jnp.concatenate([zero_blk, b], axis=1)],
                               axis=0)

    def one_chunk(ci, carry):
        r0 = pl.multiple_of(ci * chunk, chunk)
        rs = pl.ds(r0, chunk)
        fr = f_ref[rs, :]
        lsig = jnp.minimum(fr, 0.0) - jnp.log(1.0 + jnp.exp(-jnp.abs(fr)))
        c2 = log1m_lb + lsig
        logf = jnp.maximum(log_lb, c2) + jnp.log(1.0 + jnp.exp(-jnp.abs(log_lb - c2)))
        kk = (1.0 - lb) * _sigmoid(-fr)
        qq = _silu(q_ref[rs, :])
        vv = i_ref[rs, :]
        vb = vv.astype(BF16)
        bl = _cumsum_rows(tri3, logf) * LOG2E
        qe = (qq * jnp.exp2(bl)).astype(BF16)
        o = jnp.concatenate([_dot_nt(qe[:, cs], st_scr[hd].astype(BF16)) for hd, cs in enumerate(heads_cols)],
                            axis=1)
        att = [jnp.zeros((chunk, 2 * chunk), F32) for _ in range(heads // 2)]
        for h in levels:
            if h == 0:
                qf, kf, keep = qq.astype(BF16), kk.astype(BF16), rows2 == cols2
            else:
                fac = jnp.exp2(-jnp.abs(bl - _pair_reference(bl, h)))
                qf, kf = (qq * fac).astype(BF16), (kk * fac).astype(BF16)
                keep = (xor2 >= h) & (xor2 < 2 * h) & (rows2 > cols2)
            for g in range(heads // 2):
                a_h = _dot_nt(qf[:, 2 * g * dk:(2 * g + 2) * dk], block_diag(kf, g))
                att[g] = jnp.where(keep, a_h, att[g])
        o = o + jnp.concatenate([_dot(att[g].astype(BF16), block_diag(vb, g)) for g in range(heads // 2)], axis=1)
        bl_last = bl[chunk - 1:chunk]
        kd = (kk * jnp.exp2(bl_last - bl)).astype(BF16)
        decay = jnp.exp2(bl_last)
        for hd, cs in enumerate(heads_cols):
            st_scr[hd] = st_scr[hd] * decay[:, cs] + _dot_tn(vb[:, cs], kd[:, cs])
        inv = jnp.concatenate(
            [jnp.broadcast_to(lax.rsqrt(jnp.mean(o[:, cs] * o[:, cs], axis=-1, keepdims=True) + EPS), (chunk, dk))
             for cs in heads_cols], axis=1)
        o_ref[rs, :] = o * inv * jnp.concatenate([hg_ref[...]] * heads, axis=1) * _silu(zb_ref[rs, :])
        return carry

    lax.fori_loop(0, nchunks, one_chunk, 0)

    @pl.when(t == pl.num_programs(2) - 1)
    def _():
        for h in range(heads):
            s_ref[h] = st_scr[h].T


def _hgrn_branch(h3, lb_logits, hn_g, s0, layer, col_q, tc, heads):
    bsz, tlen, _ = h3.shape
    nh, dk, dv = s0.shape[1:]
    d_b = nh * dk
    chunk = CHUNK if tlen % CHUNK == 0 else tlen
    assert chunk % SUBLANES == 0 and chunk & (chunk - 1) == 0
    assert tc % chunk == 0 and tlen % tc == 0 and nh % heads == 0
    w = heads * dk
    gpb = d_b // w
    col = lambda k: pl.BlockSpec((None, tc, w), lambda bi, hi, ti: (bi, ti, (col_q + k) * gpb + hi))
    st = pl.BlockSpec((None, heads, dk, dv), lambda bi, hi, ti: (bi, hi, 0, 0))
    ne = lb_logits.shape[0]
    return pl.pallas_call(
        functools.partial(_hgrn_kernel, layer=layer, chunk=chunk, nchunks=tc // chunk, heads=heads),
        grid=(bsz, nh // heads, tlen // tc),
        in_specs=[col(0), col(1), col(2), col(3),
                  pl.BlockSpec((ne, w), lambda bi, hi, ti: (0, hi)),
                  pl.BlockSpec((1, dv), lambda bi, hi, ti: (0, 0)),
                  st],
        out_specs=[pl.BlockSpec((None, tc, w), lambda bi, hi, ti: (bi, ti, hi)), st],
        out_shape=[jax.ShapeDtypeStruct((bsz, tlen, d_b), F32), jax.ShapeDtypeStruct(s0.shape, F32)],
        scratch_shapes=[pltpu.VMEM((heads, dv, dk), F32)],
        compiler_params=_params(("parallel", "parallel", "arbitrary")),
        name="hgrn_branch",
    )(h3, h3, h3, h3, lb_logits, hn_g, s0)


def _out_ln_kernel(a_ref, b_ref, w_ref, x_ref, g_ref, bb_ref, o_ref, *, alpha):
    ka = a_ref.shape[-1]
    tm = x_ref.shape[0]
    halves = [pl.ds(i * (tm // 2), tm // 2) for i in range(2)] if tm % 16 == 0 else [pl.ds(0, tm)]
    ys = [_dot(a_ref[rs, :].astype(BF16), w_ref[0:ka, :]) + _dot(b_ref[rs, :].astype(BF16), w_ref[ka:, :])
          for rs in halves]
    for rs, y in zip(halves, ys):
        r = alpha * x_ref[rs, :] + y
        mu = jnp.mean(r, axis=-1, keepdims=True)
        rc = r - mu
        var = jnp.mean(rc * rc, axis=-1, keepdims=True)
        o_ref[rs, :] = rc * lax.rsqrt(var + EPS) * g_ref[...] + bb_ref[...]


def _out_ln(a, a_col, b, b_col, w, x, g, bb, alpha, tm):
    n, d = x.shape
    kh = w.shape[0] // 2
    assert n % tm == 0
    return pl.pallas_call(
        functools.partial(_out_ln_kernel, alpha=alpha),
        grid=(n // tm,),
        in_specs=[pl.BlockSpec((tm, kh), lambda i: (i, a_col)),
                  pl.BlockSpec((tm, kh), lambda i: (i, b_col)),
                  pl.BlockSpec(w.shape, lambda i: (0, 0)),
                  pl.BlockSpec((tm, d), lambda i: (i, 0)),
                  pl.BlockSpec((1, d), lambda i: (0, 0)),
                  pl.BlockSpec((1, d), lambda i: (0, 0))],
        out_specs=pl.BlockSpec((tm, d), lambda i: (i, 0)),
        out_shape=jax.ShapeDtypeStruct((n, d), F32),
        compiler_params=_params(("parallel",)),
        name="out_ln",
    )(a, b, w, x, g, bb)


def _lambda(lq1, lk1, lq2, lk2, lam_init):
    e1 = jnp.exp(jnp.sum(lq1[...] * lk1[...], axis=-1, keepdims=True))
    e2 = jnp.exp(jnp.sum(lq2[...] * lk2[...], axis=-1, keepdims=True))
    return e1 - e2 + lam_init


def _map_queries(q, scale):
    q = q * scale
    lane = lax.broadcasted_iota(jnp.int32, q.shape, 1)
    return jnp.concatenate([jnp.where(lane < HC_DIM, q, 0.0), jnp.where(lane >= HC_DIM, q, 0.0)],
                           axis=0).astype(BF16)


def _attn_finish(o1, o2, lam, sg_ref, z, o_ref, cols, lam_init):
    o = o1 - lam * o2
    o = o * lax.rsqrt(jnp.mean(o * o, axis=-1, keepdims=True) + EPS) * sg_ref[...] * (1.0 - lam_init)
    o_ref[:, cols] = o * _silu(z)


def _attn_prompt_kernel(q_ref, k_ref, v_ref, z_ref, lq1, lk1, lq2, lk2, sg_ref, o_ref, kb_scr, vb_scr, m_scr, acc_scr, s_scr, bias_scr,
                        *, tq, tk, lam_init):
    qi = pl.program_id(2)
    tlen = k_ref.shape[0]
    nlt = tk // LANES

    @pl.when(qi == 0)
    def _():
        kb_scr[...] = k_ref[...].astype(BF16)
        vb_scr[:, 0:LANES] = v_ref[...].astype(BF16)
        vb_scr[:, LANES:] = jnp.ones((tlen, LANES), BF16)
        q_chunk = lax.div(lax.broadcasted_iota(jnp.int32, (tq, 1), 0), CHUNK)
        k_chunk = lax.div(lax.broadcasted_iota(jnp.int32, (1, tk), 1), CHUNK)
        bias_scr[...] = jnp.where(k_chunk <= q_chunk, 0.0, -jnp.inf)

    q2 = _map_queries(q_ref[...], (HC_DIM ** -0.5) * LOG2E)
    m_scr[...] = jnp.full(m_scr.shape, -jnp.inf, F32)
    acc_scr[...] = jnp.zeros(acc_scr.shape, F32)

    def scores(kj):
        k0 = pl.multiple_of(kj * tk, tk)
        return _dot_nt(q2, kb_scr[pl.ds(k0, tk), :])

    def block(kj, masked, prefetch):
        k0 = pl.multiple_of(kj * tk, tk)
        s = s_scr[...]
        if prefetch is not None:
            s_scr[...] = scores(prefetch)
        if masked:
            bias = bias_scr[...]
            s = s + jnp.concatenate([bias, bias], axis=0)
        tiles = [s[:, t * LANES:(t + 1) * LANES] for t in range(nlt)]
        m_cur = tiles[0]
        for t in range(1, nlt):
            m_cur = jnp.maximum(m_cur, tiles[t])
        m_old = m_scr[...]
        m_new = jnp.maximum(m_old, jnp.max(m_cur, axis=-1, keepdims=True))
        alpha = jnp.exp2(m_old - m_new)
        p = jnp.concatenate([jnp.exp2(tl - m_new).astype(BF16) for tl in tiles], axis=1)
        acc_scr[...] = (jnp.concatenate([alpha, alpha], axis=1) * acc_scr[...]
                        + _dot(p, vb_scr[pl.ds(k0, tk), :]))
        m_scr[...] = m_new

    def two_blocks(i, carry):
        block(2 * i, False, 2 * i + 1)
        block(2 * i + 1, False, 2 * i + 2)
        return carry

    s_scr[...] = scores(qi)
    block(qi, True, 0)
    npre = qi - 1
    lax.fori_loop(0, npre // 2, two_blocks, 0)

    @pl.when((npre > 0) & (npre % 2 == 1))
    def _():
        block(npre - 1, False, npre)

    @pl.when(qi > 0)
    def _():
        block(qi - 1, False, None)

    lam = _lambda(lq1, lk1, lq2, lk2, lam_init)
    acc = acc_scr[...]
    _attn_finish(acc[0:tq, 0:LANES] / acc[0:tq, LANES:], acc[tq:, 0:LANES] / acc[tq:, LANES:],
                 lam, sg_ref, z_ref[...], o_ref, slice(None), lam_init)


def _attn_prompt(q3, k3, v3, z3, lq1, lk1, lq2, lk2, sg, lam_init, tq, tk):
    bsz, tlen, hd = q3.shape
    nh = hd // LANES
    assert tlen % tk == 0 and tk == tq and tq % CHUNK == 0
    qblk = pl.BlockSpec((None, tq, LANES), lambda bi, hi, qi: (bi, qi, hi))
    kblk = pl.BlockSpec((None, tlen, LANES), lambda bi, hi, qi: (bi, 0, hi))
    vec = lambda n: pl.BlockSpec((1, n), lambda bi, hi, qi: (0, 0))
    return pl.pallas_call(
        functools.partial(_attn_prompt_kernel, tq=tq, tk=tk, lam_init=lam_init),
        grid=(bsz, nh, tlen // tq),
        in_specs=[qblk, kblk, kblk, qblk, vec(HC_DIM), vec(HC_DIM), vec(HC_DIM), vec(HC_DIM), vec(LANES)],
        out_specs=qblk,
        out_shape=jax.ShapeDtypeStruct((bsz, tlen, hd), F32),
        scratch_shapes=[pltpu.VMEM((tlen, LANES), BF16), pltpu.VMEM((tlen, 2 * LANES), BF16),
                        pltpu.VMEM((2 * tq, LANES), F32), pltpu.VMEM((2 * tq, 2 * LANES), F32),
                        pltpu.VMEM((2 * tq, tk), F32), pltpu.VMEM((tq, tk), F32)],
        compiler_params=_params(("parallel", "parallel", "arbitrary")),
        name="attn_prompt",
    )(q3, k3, v3, z3, lq1, lk1, lq2, lk2, sg)


def _attn_sample_kernel(q_ref, ck_ref, cv_ref, kn_ref, vn_ref, z_ref, lq1, lk1, lq2, lk2, sg_ref, o_ref,
                        m_scr, l_scr, acc_scr, *, lam_init, past):
    kb = pl.program_id(2)
    tlen = q_ref.shape[0]
    tkc, heads = ck_ref.shape[0], ck_ref.shape[1]
    per_head = 2 * tlen
    ncol = heads * per_head

    @pl.when(kb == 0)
    def _():
        m_scr[...] = jnp.full(m_scr.shape, -jnp.inf, F32)
        l_scr[...] = jnp.zeros(l_scr.shape, F32)
        acc_scr[...] = jnp.zeros(acc_scr.shape, F32)

    head_cols = [slice(h * LANES, (h + 1) * LANES) for h in range(heads)]
    scale = (HC_DIM ** -0.5) * LOG2E
    qw = jnp.concatenate([_map_queries(q_ref[:, cs], scale) for cs in head_cols], axis=0)
    col = lax.broadcasted_iota(jnp.int32, (1, ncol), 1)
    col_head = lax.div(col, per_head)
    q_chunk = lax.div(past + lax.rem(col, tlen), CHUNK)

    def update(k_ref, v_ref, row_head, row_pos):
        k, v = k_ref().astype(BF16), v_ref().astype(BF16)
        n, cyc = k.shape[0], row_head.shape[0]
        own = jnp.where(row_head == col_head, 0.0, -jnp.inf)
        pieces = SAMPLE_ROW_PIECES if n % (SAMPLE_ROW_PIECES * cyc) == 0 else 1
        step = n // pieces
        scores = lambda i: _dot_nt(k[i * step:(i + 1) * step], qw)
        s_next = scores(0)
        for i in range(pieces):
            s = s_next
            if i + 1 < pieces:
                s_next = scores(i + 1)
            s = (s.reshape(step // cyc, cyc, ncol) + own[None]).reshape(step, ncol)
            if row_pos is not None:
                s = jnp.where(lax.div(row_pos[i * step:(i + 1) * step], CHUNK) <= q_chunk, s, -jnp.inf)
            m_old = m_scr[...]
            m_new = jnp.maximum(m_old, jnp.max(s, axis=0, keepdims=True))
            alpha = jnp.exp2(m_old - m_new)
            p = jnp.exp2(s - m_new)
            l_scr[...] = alpha * l_scr[...] + jnp.sum(p, axis=0, keepdims=True)
            acc_scr[...] = alpha * acc_scr[...] + _dot_tn(v[i * step:(i + 1) * step], p.astype(BF16))
            m_scr[...] = m_new

    cached_k = lambda: ck_ref[...].reshape(tkc * heads, LANES)
    cached_v = lambda: cv_ref[...].reshape(tkc * heads, LANES)
    cached_head = lax.broadcasted_iota(jnp.int32, (heads, 1), 0)
    all_visible = (kb * tkc + tkc - 1) // CHUNK <= past // CHUNK

    @pl.when(all_visible)
    def _():
        update(cached_k, cached_v, cached_head, None)

    @pl.when(jnp.logical_not(all_visible))
    def _():
        row = lax.broadcasted_iota(jnp.int32, (tkc * heads, 1), 0)
        update(cached_k, cached_v, cached_head, kb * tkc + lax.div(row, heads))

    @pl.when(kb == pl.num_programs(2) - 1)
    def _():
        row_n = lax.broadcasted_iota(jnp.int32, (tlen * heads, 1), 0)
        update(lambda: jnp.concatenate([kn_ref[:, cs] for cs in head_cols], axis=0),
               lambda: jnp.concatenate([vn_ref[:, cs] for cs in head_cols], axis=0),
               lax.div(row_n, tlen), past + lax.rem(row_n, tlen))
        o = (acc_scr[...] / l_scr[...]).T
        lam = _lambda(lq1, lk1, lq2, lk2, lam_init)
        for h, cs in enumerate(head_cols):
            o_h = o[h * per_head:(h + 1) * per_head]
            _attn_finish(o_h[0:tlen], o_h[tlen:], lam, sg_ref, z_ref[:, cs], o_ref, cs, lam_init)


def _attn_sample(q3, ck5, cv5, layer, kn3, vn3, z3, lq1, lk1, lq2, lk2, sg, lam_init, tkc, heads):
    bsz, tlen, hd = q3.shape
    past, nh = ck5.shape[2], ck5.shape[3]
    assert past % tkc == 0 and nh % heads == 0
    ncol = heads * 2 * tlen
    new = pl.BlockSpec((None, tlen, heads * LANES), lambda bi, hi, ki: (bi, 0, hi))
    old = pl.BlockSpec((None, None, tkc, heads, LANES), lambda bi, hi, ki: (layer, bi, ki, hi, 0))
    vec = lambda n: pl.BlockSpec((1, n), lambda bi, hi, ki: (0, 0))
    stat = pltpu.VMEM((1, ncol), F32)
    return pl.pallas_call(
        functools.partial(_attn_sample_kernel, lam_init=lam_init, past=past),
        grid=(bsz, nh // heads, past // tkc),
        in_specs=[new, old, old, new, new, new, vec(HC_DIM), vec(HC_DIM), vec(HC_DIM), vec(HC_DIM), vec(LANES)],
        out_specs=new,
        out_shape=jax.ShapeDtypeStruct((bsz, tlen, hd), F32),
        scratch_shapes=[stat, stat, pltpu.VMEM((LANES, ncol), F32)],
        compiler_params=_params(("parallel", "parallel", "arbitrary")),
        name="attn_sample",
    )(q3, ck5, cv5, kn3, vn3, z3, lq1, lk1, lq2, lk2, sg)


def _row_tile(n, pref):
    while n % pref:
        pref //= 2
    return pref


def _even_layer(x3, conv_buf, s0, w_in, w_out, cw, cb, cg, cbb, lb_logits, hn_g, pg, pb, layer, alpha):
    bsz, tlen, d = x3.shape
    n = bsz * tlen
    x2 = x3.reshape(n, d)
    d_a = cw.shape[-1]
    (h,) = _linear(x2, w_in, 1, _row_tile(n, 1024), 1024)
    h3 = h.reshape(bsz, tlen, -1)
    buf = jnp.pad(conv_buf, ((0, 0), (HIST - (CONV_W - 1), 0), (0, 0)))
    out_a, nbuf = _conv_branch(h3, buf, cw, cb, cg, cbb, _row_tile(tlen, 256))
    chunk = CHUNK if tlen % CHUNK == 0 else tlen
    tc = _row_tile(tlen, 512) if tlen % CHUNK == 0 else chunk
    d_b = s0.shape[1] * s0.shape[2]
    out_b, s_new = _hgrn_branch(h3, lb_logits, hn_g, s0, layer, 3 * d_a // d_b, tc, HGRN_HEADS_PER_STEP)
    xn = _out_ln(out_a.reshape(n, -1), 0, out_b.reshape(n, -1), 0, w_out, x2, pg, pb, alpha, _row_tile(n, 512))
    return xn.reshape(bsz, tlen, d), nbuf[:, HIST - (CONV_W - 1):], s_new


def _odd_layer(x3, caches, w_in, w_out, lq1, lk1, lq2, lk2, sg, pg, pb, lam_init, alpha):
    bsz, tlen, d = x3.shape
    n = bsz * tlen
    x2 = x3.reshape(n, d)
    d_c = w_in.shape[1] // 4
    tm = _row_tile(n, 1024)
    q, k, v, z = [y.reshape(bsz, tlen, d_c) for y in _linear(x2, w_in, 4, tm, 1024)]
    if caches is None:
        tk = _row_tile(tlen, 512)
        o = _attn_prompt(q, k, v, z, lq1, lk1, lq2, lk2, sg, lam_init, tk, tk)
    else:
        ck, cv, layer = caches
        o = _attn_sample(q, ck, cv, layer, k, v, z, lq1, lk1, lq2, lk2, sg, lam_init,
                         _row_tile(ck.shape[2], 1024), SUBLANES)
    o2 = o.reshape(n, d_c)
    xn = _out_ln(o2, 0, o2, 1, w_out, x2, pg, pb, alpha, _row_tile(n, 512))
    nh = d_c // LANES
    return xn.reshape(bsz, tlen, d), k.reshape(bsz, tlen, nh, LANES), v.reshape(bsz, tlen, nh, LANES)


def kernel(x_prompt, x_sample, state_conv, state_hgrn, cache_k, cache_v, w_in_even, w_out_even, conv_w, conv_b,
           conv_ln_g, conv_ln_b, hgrn_lb_logits, hgrn_norm_g, w_in_odd, w_out_odd, lam_q1, lam_k1, lam_q2, lam_k2,
           subln_g, post_ln_g, post_ln_b):
    depth = post_ln_g.shape[0]
    alpha = (2.0 * depth) ** 0.25
    xp, xs = x_prompt, x_sample
    bp = xp.shape[0]
    row = lambda a: a.reshape(1, -1)
    outs = {name: [] for name in ("conv_p", "hgrn_p", "k_p", "v_p", "conv_s", "hgrn_s", "k_s", "v_s")}
    for l in range(depth):
        pg, pb = row(post_ln_g[l]), row(post_ln_b[l])
        if l % 2 == 0:
            e = l // 2
            prm = (w_in_even[e].astype(BF16), w_out_even[e].astype(BF16), conv_w[e], row(conv_b[e]),
                   row(conv_ln_g[e]), row(conv_ln_b[e]), hgrn_lb_logits, row(hgrn_norm_g[e]), pg, pb, e, alpha)
            buf0 = jnp.zeros((bp,) + state_conv.shape[2:], F32)
            s0 = jnp.zeros((bp,) + state_hgrn.shape[2:], F32)
            xp, cb_p, s_p = _even_layer(xp, buf0, s0, *prm)
            xs, cb_s, s_s = _even_layer(xs, state_conv[e], state_hgrn[e], *prm)
            outs["conv_p"].append(cb_p); outs["hgrn_p"].append(s_p)
            outs["conv_s"].append(cb_s); outs["hgrn_s"].append(s_s)
        else:
            o = l // 2
            lam_init = 0.8 - 0.6 * math.exp(-0.3 * l)
            prm = (w_in_odd[o].astype(BF16), w_out_odd[o].astype(BF16), row(lam_q1[o]), row(lam_k1[o]),
                   row(lam_q2[o]), row(lam_k2[o]), row(subln_g[o]), pg, pb, lam_init, alpha)
            xp, kn_p, vn_p = _odd_layer(xp, None, *prm)
            xs, kn_s, vn_s = _odd_layer(xs, (cache_k, cache_v, o), *prm)
            outs["k_p"].append(kn_p); outs["v_p"].append(vn_p)
            outs["k_s"].append(kn_s); outs["v_s"].append(vn_s)
    st = lambda name: jnp.stack(outs[name])
    return (xp, xs, st("conv_p"), st("hgrn_p"), st("k_p"), st("v_p"),
            st("conv_s"), st("hgrn_s"), st("k_s"), st("v_s"))
```

```python
import functools
import math

import jax
import jax.numpy as jnp
from jax import lax
from jax.experimental import pallas as pl
from jax.experimental.pallas import tpu as pltpu

F32 = jnp.float32
BF16 = jnp.bfloat16

CHUNK = 64
SUBLANES = 8
CONV_W = 31
HIST = 32
HB_DK = 128
HC_DIM = 64
LANES = 128
EPS = 1e-5
LOG2E = math.log2(math.e)
SAMPLE_ROW_PIECES = 4
HGRN_HEADS_PER_STEP = 8
VMEM_LIMIT = 48 * 1024 * 1024


def _params(sem):
    return pltpu.CompilerParams(dimension_semantics=sem, vmem_limit_bytes=VMEM_LIMIT)


def _dot(a, b):
    return jnp.dot(a, b, preferred_element_type=F32)


def _dot_nt(a, b):
    return lax.dot_general(a, b, (((1,), (1,)), ((), ())), preferred_element_type=F32)


def _dot_tn(a, b):
    return lax.dot_general(a, b, (((0,), (0,)), ((), ())), preferred_element_type=F32)


def _sigmoid(x):
    return 1.0 / (1.0 + jnp.exp(-x))


def _silu(x):
    return x * _sigmoid(x)


def _linear_kernel(x_ref, w_ref, *o_refs, blocks_per_out):
    j = pl.program_id(1)
    if len(o_refs) == 1:
        o_refs[0][...] = _dot(x_ref[...].astype(BF16), w_ref[...])
        return
    for k, o_ref in enumerate(o_refs):
        @pl.when((j >= k * blocks_per_out) & (j < (k + 1) * blocks_per_out))
        def _(o_ref=o_ref):
            o_ref[...] = _dot(x_ref[...].astype(BF16), w_ref[...])


def _linear(x, w, nout, tm, tn):
    n, k = x.shape
    width = w.shape[1] // nout
    assert n % tm == 0 and width % tn == 0 and width * nout == w.shape[1]
    bpo = width // tn

    def out_map(g):
        return lambda i, j: (i, jnp.clip(j - g * bpo, 0, bpo - 1))

    return pl.pallas_call(
        functools.partial(_linear_kernel, blocks_per_out=bpo),
        grid=(n // tm, nout * bpo),
        in_specs=[pl.BlockSpec((tm, k), lambda i, j: (i, 0)),
                  pl.BlockSpec((k, tn), lambda i, j: (0, j))],
        out_specs=[pl.BlockSpec((tm, tn), out_map(g)) for g in range(nout)],
        out_shape=[jax.ShapeDtypeStruct((n, width), F32) for _ in range(nout)],
        compiler_params=_params(("parallel", "arbitrary")),
        name="linear",
    )(x, w)


def _conv_kernel(a_ref, ga_ref, za_ref, buf_ref, w_ref, cb_ref, g_ref, b_ref, out_ref, nb_ref, scr, c_scr, sh_scr,
                 *, tt, rb):
    t = pl.program_id(1)
    nch = a_ref.shape[-1] // LANES

    @pl.when(t == 0)
    def _():
        scr[0:HIST, :] = buf_ref[...]

    @pl.when(t > 0)
    def _():
        scr[0:HIST, :] = scr[tt:tt + HIST, :]

    scr[HIST:HIST + tt, :] = a_ref[...] * _sigmoid(ga_ref[...])

    def row_block(r, carry):
        r0 = pl.multiple_of(r * rb, rb)
        for c in range(nch):
            cs = slice(c * LANES, (c + 1) * LANES)
            slab = scr[pl.ds(r0, rb + HIST), cs]
            for shift in range(SUBLANES):
                n = (rb + HIST - shift) // SUBLANES * SUBLANES
                sh_scr[shift, 0:n, :] = slab[shift:shift + n, :]
            acc = jnp.zeros((rb, LANES), F32)
            for j in range(CONV_W):
                shift = (j + 2) % SUBLANES
                off = j + 2 - shift
                acc = acc + w_ref[j:j + 1, cs] * sh_scr[shift, off:off + rb, :]
            c_scr[pl.ds(r0, rb), cs] = acc + cb_ref[:, cs]
        return carry

    lax.fori_loop(0, tt // rb, row_block, 0)

    cv = c_scr[...]
    mu = jnp.mean(cv, axis=-1, keepdims=True)
    xc = cv - mu
    var = jnp.mean(xc * xc, axis=-1, keepdims=True)
    y = xc * lax.rsqrt(var + EPS) * g_ref[...] + b_ref[...]
    out_ref[...] = _silu(y) * _silu(za_ref[...])

    @pl.when(t == pl.num_programs(1) - 1)
    def _():
        nb_ref[...] = scr[tt:tt + HIST, :]


def _conv_branch(h3, buf, cw, cb, g, b, tt):
    bsz, tlen, _ = h3.shape
    c = cw.shape[-1]
    assert tlen % tt == 0
    rb = min(64, tt)
    assert tt % rb == 0
    row = lambda col: pl.BlockSpec((None, tt, c), lambda bi, ti: (bi, ti, col))
    vec = lambda r: pl.BlockSpec((r, c), lambda bi, ti: (0, 0))
    hist = pl.BlockSpec((None, HIST, c), lambda bi, ti: (bi, 0, 0))
    return pl.pallas_call(
        functools.partial(_conv_kernel, tt=tt, rb=rb),
        grid=(bsz, tlen // tt),
        in_specs=[row(0), row(1), row(2), hist, vec(CONV_W), vec(1), vec(1), vec(1)],
        out_specs=[pl.BlockSpec((None, tt, c), lambda bi, ti: (bi, ti, 0)), hist],
        out_shape=[jax.ShapeDtypeStruct((bsz, tlen, c), F32), jax.ShapeDtypeStruct((bsz, HIST, c), F32)],
        scratch_shapes=[pltpu.VMEM((tt + HIST, c), F32), pltpu.VMEM((tt, c), F32),
                        pltpu.VMEM((SUBLANES, rb + HIST, LANES), F32)],
        compiler_params=_params(("parallel", "arbitrary")),
        name="conv_branch",
    )(h3, h3, h3, buf, cw, cb, g, b)


def _cumsum_rows(tri3, g):
    g0 = g.astype(BF16)
    r1 = g - g0.astype(F32)
    g1 = r1.astype(BF16)
    g2 = (r1 - g1.astype(F32)).astype(BF16)
    return _dot(tri3, jnp.concatenate([g0, g1, g2], axis=0))


def _pair_reference(x, h):
    nrows, width = x.shape
    if h >= SUBLANES:
        parts = [jnp.broadcast_to(x[p + h - 1:p + h], (2 * h, width)) for p in range(0, nrows, 2 * h)]
        return parts[0] if len(parts) == 1 else jnp.concatenate(parts, axis=0)
    x3 = x.reshape(nrows // SUBLANES, SUBLANES, width)
    sub = lax.broadcasted_iota(jnp.int32, (1, SUBLANES, 1), 1)
    ref = None
    for p in range(SUBLANES - 2 * h, -1, -2 * h):
        row = x3[:, p + h - 1:p + h, :]
        ref = row if ref is None else jnp.where(sub < p + 2 * h, row, ref)
    return jnp.broadcast_to(ref, x3.shape).reshape(nrows, width)


def _hgrn_kernel(q_ref, f_ref, i_ref, zb_ref, lbl_ref, hg_ref, s0_ref, o_ref, s_ref, st_scr,
                 *, layer, chunk, nchunks, heads):
    t = pl.program_id(2)
    dk = HB_DK

    @pl.when(t == 0)
    def _():
        for h in range(heads):
            st_scr[h] = s0_ref[h].T

    lg = lbl_ref[...]
    ex = jnp.exp(lg - jnp.max(lg, axis=0, keepdims=True))
    sm = ex / jnp.sum(ex, axis=0, keepdims=True)
    lb = jnp.zeros_like(sm[0:1])
    for r in range(1, layer + 1):
        lb = lb + sm[r:r + 1]
    log_lb = jnp.log(lb)
    log1m_lb = jnp.log1p(-lb)

    rows = lax.broadcasted_iota(jnp.int32, (chunk, chunk), 0)
    cols = lax.broadcasted_iota(jnp.int32, (chunk, chunk), 1)
    tri = jnp.where(rows >= cols, 1.0, 0.0).astype(BF16)
    tri3 = jnp.concatenate([tri, tri, tri], axis=1)
    rows2 = lax.broadcasted_iota(jnp.int32, (chunk, 2 * chunk), 0)
    cols2 = lax.rem(lax.broadcasted_iota(jnp.int32, (chunk, 2 * chunk), 1), chunk)
    xor2 = jnp.bitwise_xor(rows2, cols2)
    levels = [0]
    h = 1
    while h < chunk:
        levels.append(h)
        h *= 2
    heads_cols = [slice(hd * dk, (hd + 1) * dk) for hd in range(heads)]
    zero_blk = jnp.zeros((chunk, dk), BF16)

    def block_diag(x, g):
        a, b = x[:, heads_cols[2 * g]], x[:, heads_cols[2 * g + 1]]
        return jnp.concatenate([jnp.concatenate([a, zero_blk], axis=1), jnp.concatenate([zero_blk, b], axis=1)],
                               axis=0)

    def one_chunk(ci, carry):
        r0 = pl.multiple_of(ci * chunk, chunk)
        rs = pl.ds(r0, chunk)
        fr = f_ref[rs, :]
        lsig = jnp.minimum(fr, 0.0) - jnp.log(1.0 + jnp.exp(-jnp.abs(fr)))
        c2 = log1m_lb + lsig
        logf = jnp.maximum(log_lb, c2) + jnp.log(1.0 + jnp.exp(-jnp.abs(log_lb - c2)))
        kk = (1.0 - lb) * _sigmoid(-fr)
        qq = _silu(q_ref[rs, :])
        vv = i_ref[rs, :]
        vb = vv.astype(BF16)
        bl = _cumsum_rows(tri3, logf) * LOG2E
        qe = (qq * jnp.exp2(bl)).astype(BF16)
        o = jnp.concatenate([_dot_nt(qe[:, cs], st_scr[hd].astype(BF16)) for hd, cs in enumerate(heads_cols)],
                            axis=1)
        att = [jnp.zeros((chunk, 2 * chunk), F32) for _ in range(heads // 2)]
        for h in levels:
            if h == 0:
                qf, kf, keep = qq.astype(BF16), kk.astype(BF16), rows2 == cols2
            else:
                fac = jnp.exp2(-jnp.abs(bl - _pair_reference(bl, h)))
                qf, kf = (qq * fac).astype(BF16), (kk * fac).astype(BF16)
                keep = (xor2 >= h) & (xor2 < 2 * h) & (rows2 > cols2)
            for g in range(heads // 2):
                a_h = _dot_nt(qf[:, 2 * g * dk:(2 * g + 2) * dk], block_diag(kf, g))
                att[g] = jnp.where(keep, a_h, att[g])
        o = o + jnp.concatenate([_dot(att[g].astype(BF16), block_diag(vb, g)) for g in range(heads // 2)], axis=1)
        bl_last = bl[chunk - 1:chunk]
        kd = (kk * jnp.exp2(bl_last - bl)).astype(BF16)
        decay = jnp.exp2(bl_last)
        for hd, cs in enumerate(heads_cols):
            st_scr[hd] = st_scr[hd] * decay[:, cs] + _dot_tn(vb[:, cs], kd[:, cs])
        inv = jnp.concatenate(
            [jnp.broadcast_to(lax.rsqrt(jnp.mean(o[:, cs] * o[:, cs], axis=-1, keepdims=True) + EPS), (chunk, dk))
             for cs in heads_cols], axis=1)
        o_ref[rs, :] = o * inv * jnp.concatenate([hg_ref[...]] * heads, axis=1) * _silu(zb_ref[rs, :])
        return carry

    lax.fori_loop(0, nchunks, one_chunk, 0)

    @pl.when(t == pl.num_programs(2) - 1)
    def _():
        for h in range(heads):
            s_ref[h] = st_scr[h].T


def _hgrn_branch(h3, lb_logits, hn_g, s0, layer, col_q, tc, heads):
    bsz, tlen, _ = h3.shape
    nh, dk, dv = s0.shape[1:]
    d_b = nh * dk
    chunk = CHUNK if tlen % CHUNK == 0 else tlen
    assert chunk % SUBLANES == 0 and chunk & (chunk - 1) == 0
    assert tc % chunk == 0 and tlen % tc == 0 and nh % heads == 0
    w = heads * dk
    gpb = d_b // w
    col = lambda k: pl.BlockSpec((None, tc, w), lambda bi, hi, ti: (bi, ti, (col_q + k) * gpb + hi))
    st = pl.BlockSpec((None, heads, dk, dv), lambda bi, hi, ti: (bi, hi, 0, 0))
    ne = lb_logits.shape[0]
    return pl.pallas_call(
        functools.partial(_hgrn_kernel, layer=layer, chunk=chunk, nchunks=tc // chunk, heads=heads),
        grid=(bsz, nh // heads, tlen // tc),
        in_specs=[col(0), col(1), col(2), col(3),
                  pl.BlockSpec((ne, w), lambda bi, hi, ti: (0, hi)),
                  pl.BlockSpec((1, dv), lambda bi, hi, ti: (0, 0)),
                  st],
        out_specs=[pl.BlockSpec((None, tc, w), lambda bi, hi, ti: (bi, ti, hi)), st],
        out_shape=[jax.ShapeDtypeStruct((bsz, tlen, d_b), F32), jax.ShapeDtypeStruct(s0.shape, F32)],
        scratch_shapes=[pltpu.VMEM((heads, dv, dk), F32)],
        compiler_params=_params(("parallel", "parallel", "arbitrary")),
        name="hgrn_branch",
    )(h3, h3, h3, h3, lb_logits, hn_g, s0)


def _out_ln_kernel(a_ref, b_ref, w_ref, x_ref, g_ref, bb_ref, o_ref, *, alpha):
    ka = a_ref.shape[-1]
    tm = x_ref.shape[0]
    halves = [pl.ds(i * (tm // 2), tm // 2) for i in range(2)] if tm % 16 == 0 else [pl.ds(0, tm)]
    ys = [_dot(a_ref[rs, :].astype(BF16), w_ref[0:ka, :]) + _dot(b_ref[rs, :].astype(BF16), w_ref[ka:, :])
          for rs in halves]
    for rs, y in zip(halves, ys):
        r = alpha * x_ref[rs, :] + y
        mu = jnp.mean(r, axis=-1, keepdims=True)
        rc = r - mu
        var = jnp.mean(rc * rc, axis=-1, keepdims=True)
        o_ref[rs, :] = rc * lax.rsqrt(var + EPS) * g_ref[...] + bb_ref[...]


def _out_ln(a, a_col, b, b_col, w, x, g, bb, alpha, tm):
    n, d = x.shape
    kh = w.shape[0] // 2
    assert n % tm == 0
    return pl.pallas_call(
        functools.partial(_out_ln_kernel, alpha=alpha),
        grid=(n // tm,),
        in_specs=[pl.BlockSpec((tm, kh), lambda i: (i, a_col)),
                  pl.BlockSpec((tm, kh), lambda i: (i, b_col)),
                  pl.BlockSpec(w.shape, lambda i: (0, 0)),
                  pl.BlockSpec((tm, d), lambda i: (i, 0)),
                  pl.BlockSpec((1, d), lambda i: (0, 0)),
                  pl.BlockSpec((1, d), lambda i: (0, 0))],
        out_specs=pl.BlockSpec((tm, d), lambda i: (i, 0)),
        out_shape=jax.ShapeDtypeStruct((n, d), F32),
        compiler_params=_params(("parallel",)),
        name="out_ln",
    )(a, b, w, x, g, bb)


def _lambda(lq1, lk1, lq2, lk2, lam_init):
    e1 = jnp.exp(jnp.sum(lq1[...] * lk1[...], axis=-1, keepdims=True))
    e2 = jnp.exp(jnp.sum(lq2[...] * lk2[...], axis=-1, keepdims=True))
    return e1 - e2 + lam_init


def _map_queries(q, scale):
    q = q * scale
    lane = lax.broadcasted_iota(jnp.int32, q.shape, 1)
    return jnp.concatenate([jnp.where(lane < HC_DIM, q, 0.0), jnp.where(lane >= HC_DIM, q, 0.0)],
                           axis=0).astype(BF16)


def _attn_finish(o1, o2, lam, sg_ref, z, o_ref, cols, lam_init):
    o = o1 - lam * o2
    o = o * lax.rsqrt(jnp.mean(o * o, axis=-1, keepdims=True) + EPS) * sg_ref[...] * (1.0 - lam_init)
    o_ref[:, cols] = o * _silu(z)


def _attn_prompt_kernel(q_ref, k_ref, v_ref, z_ref, lq1, lk1, lq2, lk2, sg_ref, o_ref,
                        kb_scr, vb_scr, m_scr, acc_scr, s_scr, bias_scr, *, tq, tk, lam_init):
    qi = pl.program_id(2)
    tlen = k_ref.shape[0]
    nlt = tk // LANES

    @pl.when(qi == 0)
    def _():
        kb_scr[...] = k_ref[...].astype(BF16)
        vb_scr[:, 0:LANES] = v_ref[...].astype(BF16)
        vb_scr[:, LANES:] = jnp.ones((tlen, LANES), BF16)
        q_chunk = lax.div(lax.broadcasted_iota(jnp.int32, (tq, 1), 0), CHUNK)
        k_chunk = lax.div(lax.broadcasted_iota(jnp.int32, (1, tk), 1), CHUNK)
        bias_scr[...] = jnp.where(k_chunk <= q_chunk, 0.0, -jnp.inf)

    q2 = _map_queries(q_ref[...], (HC_DIM ** -0.5) * LOG2E)
    m_scr[...] = jnp.full(m_scr.shape, -jnp.inf, F32)
    acc_scr[...] = jnp.zeros(acc_scr.shape, F32)

    def scores(kj):
        k0 = pl.multiple_of(kj * tk, tk)
        return _dot_nt(q2, kb_scr[pl.ds(k0, tk), :])

    def consume(s, rows, k_start, nkeys):
        tiles = [s[:, t * LANES:(t + 1) * LANES] for t in range(nkeys // LANES)]
        m_cur = tiles[0]
        for tl in tiles[1:]:
            m_cur = jnp.maximum(m_cur, tl)
        m_old = m_scr[rows, :]
        m_new = jnp.maximum(m_old, jnp.max(m_cur, axis=-1, keepdims=True))
        alpha = jnp.exp2(m_old - m_new)
        p = jnp.concatenate([jnp.exp2(tl - m_new).astype(BF16) for tl in tiles], axis=1)
        acc_scr[rows, :] = (jnp.concatenate([alpha, alpha], axis=1) * acc_scr[rows, :]
                            + _dot(p, vb_scr[pl.ds(k_start, nkeys), :]))
        m_scr[rows, :] = m_new

    def block(kj, prefetch):
        s = s_scr[...]
        if prefetch is not None:
            s_scr[...] = scores(prefetch)
        consume(s, slice(None), pl.multiple_of(kj * tk, tk), tk)

    def diagonal_block():
        half = tq // 2
        k_lo = pl.multiple_of(qi * tk, tk)
        k_hi = pl.multiple_of(qi * tk + half, half)
        bias = bias_scr[...]
        s_lo = _dot_nt(q2, kb_scr[pl.ds(k_lo, half), :]) + jnp.concatenate([bias[:, 0:half]] * 2, axis=0)
        q_hi = jnp.concatenate([q2[half:tq], q2[tq + half:]], axis=0)
        s_hi = _dot_nt(q_hi, kb_scr[pl.ds(k_hi, half), :]) + jnp.concatenate([bias[half:, half:]] * 2, axis=0)
        s_scr[...] = scores(0)
        consume(s_lo, slice(None), k_lo, half)
        consume(s_hi[0:half], slice(half, tq), k_hi, half)
        consume(s_hi[half:], slice(tq + half, 2 * tq), k_hi, half)

    def two_blocks(i, carry):
        block(2 * i, 2 * i + 1)
        block(2 * i + 1, 2 * i + 2)
        return carry

    diagonal_block()
    npre = qi - 1
    lax.fori_loop(0, npre // 2, two_blocks, 0)

    @pl.when((npre > 0) & (npre % 2 == 1))
    def _():
        block(npre - 1, npre)

    @pl.when(qi > 0)
    def _():
        block(qi - 1, None)

    lam = _lambda(lq1, lk1, lq2, lk2, lam_init)
    acc = acc_scr[...]
    _attn_finish(acc[0:tq, 0:LANES] / acc[0:tq, LANES:], acc[tq:, 0:LANES] / acc[tq:, LANES:],
                 lam, sg_ref, z_ref[...], o_ref, slice(None), lam_init)


def _attn_prompt(q3, k3, v3, z3, lq1, lk1, lq2, lk2, sg, lam_init, tq, tk):
    bsz, tlen, hd = q3.shape
    nh = hd // LANES
    assert tlen % tk == 0 and tk == tq and tq % CHUNK == 0
    qblk = pl.BlockSpec((None, tq, LANES), lambda bi, hi, qi: (bi, qi, hi))
    kblk = pl.BlockSpec((None, tlen, LANES), lambda bi, hi, qi: (bi, 0, hi))
    vec = lambda n: pl.BlockSpec((1, n), lambda bi, hi, qi: (0, 0))
    return pl.pallas_call(
        functools.partial(_attn_prompt_kernel, tq=tq, tk=tk, lam_init=lam_init),
        grid=(bsz, nh, tlen // tq),
        in_specs=[qblk, kblk, kblk, qblk, vec(HC_DIM), vec(HC_DIM), vec(HC_DIM), vec(HC_DIM), vec(LANES)],
        out_specs=qblk,
        out_shape=jax.ShapeDtypeStruct((bsz, tlen, hd), F32),
        scratch_shapes=[pltpu.VMEM((tlen, LANES), BF16), pltpu.VMEM((tlen, 2 * LANES), BF16),
                        pltpu.VMEM((2 * tq, LANES), F32), pltpu.VMEM((2 * tq, 2 * LANES), F32),
                        pltpu.VMEM((2 * tq, tk), F32), pltpu.VMEM((tq, tk), F32)],
        compiler_params=_params(("parallel", "parallel", "arbitrary")),
        name="attn_prompt",
    )(q3, k3, v3, z3, lq1, lk1, lq2, lk2, sg)


def _attn_sample_kernel(q_ref, ck_ref, cv_ref, kn_ref, vn_ref, z_ref, lq1, lk1, lq2, lk2, sg_ref, o_ref,
                        m_scr, l_scr, acc_scr, *, lam_init, past):
    kb = pl.program_id(2)
    tlen = q_ref.shape[0]
    tkc, heads = ck_ref.shape[0], ck_ref.shape[1]
    per_head = 2 * tlen
    ncol = heads * per_head

    @pl.when(kb == 0)
    def _():
        m_scr[...] = jnp.full(m_scr.shape, -jnp.inf, F32)
        l_scr[...] = jnp.zeros(l_scr.shape, F32)
        acc_scr[...] = jnp.zeros(acc_scr.shape, F32)

    head_cols = [slice(h * LANES, (h + 1) * LANES) for h in range(heads)]
    scale = (HC_DIM ** -0.5) * LOG2E
    qw = jnp.concatenate([_map_queries(q_ref[:, cs], scale) for cs in head_cols], axis=0)
    col = lax.broadcasted_iota(jnp.int32, (1, ncol), 1)
    col_head = lax.div(col, per_head)
    q_chunk = lax.div(past + lax.rem(col, tlen), CHUNK)

    def update(k_ref, v_ref, row_head, row_pos):
        k, v = k_ref().astype(BF16), v_ref().astype(BF16)
        n, cyc = k.shape[0], row_head.shape[0]
        own = jnp.where(row_head == col_head, 0.0, -jnp.inf)
        pieces = SAMPLE_ROW_PIECES if n % (SAMPLE_ROW_PIECES * cyc) == 0 else 1
        step = n // pieces
        scores = lambda i: _dot_nt(k[i * step:(i + 1) * step], qw)
        s_next = scores(0)
        for i in range(pieces):
            s = s_next
            if i + 1 < pieces:
                s_next = scores(i + 1)
            s = (s.reshape(step // cyc, cyc, ncol) + own[None]).reshape(step, ncol)
            if row_pos is not None:
                s = jnp.where(lax.div(row_pos[i * step:(i + 1) * step], CHUNK) <= q_chunk, s, -jnp.inf)
            m_old = m_scr[...]
            m_new = jnp.maximum(m_old, jnp.max(s, axis=0, keepdims=True))
            alpha = jnp.exp2(m_old - m_new)
            p = jnp.exp2(s - m_new)
            l_scr[...] = alpha * l_scr[...] + jnp.sum(p, axis=0, keepdims=True)
            acc_scr[...] = alpha * acc_scr[...] + _dot_tn(v[i * step:(i + 1) * step], p.astype(BF16))
            m_scr[...] = m_new

    cached_k = lambda: ck_ref[...].reshape(tkc * heads, LANES)
    cached_v = lambda: cv_ref[...].reshape(tkc * heads, LANES)
    cached_head = lax.broadcasted_iota(jnp.int32, (heads, 1), 0)
    all_visible = (kb * tkc + tkc - 1) // CHUNK <= past // CHUNK

    @pl.when(all_visible)
    def _():
        update(cached_k, cached_v, cached_head, None)

    @pl.when(jnp.logical_not(all_visible))
    def _():
        row = lax.broadcasted_iota(jnp.int32, (tkc * heads, 1), 0)
        update(cached_k, cached_v, cached_head, kb * tkc + lax.div(row, heads))

    @pl.when(kb == pl.num_programs(2) - 1)
    def _():
        row_n = lax.broadcasted_iota(jnp.int32, (tlen * heads, 1), 0)
        update(lambda: jnp.concatenate([kn_ref[:, cs] for cs in head_cols], axis=0),
               lambda: jnp.concatenate([vn_ref[:, cs] for cs in head_cols], axis=0),
               lax.div(row_n, tlen), past + lax.rem(row_n, tlen))
        o = (acc_scr[...] / l_scr[...]).T
        lam = _lambda(lq1, lk1, lq2, lk2, lam_init)
        for h, cs in enumerate(head_cols):
            o_h = o[h * per_head:(h + 1) * per_head]
            _attn_finish(o_h[0:tlen], o_h[tlen:], lam, sg_ref, z_ref[:, cs], o_ref, cs, lam_init)


def _attn_sample(q3, ck5, cv5, layer, kn3, vn3, z3, lq1, lk1, lq2, lk2, sg, lam_init, tkc, heads):
    bsz, tlen, hd = q3.shape
    past, nh = ck5.shape[2], ck5.shape[3]
    assert past % tkc == 0 and nh % heads == 0
    ncol = heads * 2 * tlen
    new = pl.BlockSpec((None, tlen, heads * LANES), lambda bi, hi, ki: (bi, 0, hi))
    old = pl.BlockSpec((None, None, tkc, heads, LANES), lambda bi, hi, ki: (layer, bi, ki, hi, 0))
    vec = lambda n: pl.BlockSpec((1, n), lambda bi, hi, ki: (0, 0))
    stat = pltpu.VMEM((1, ncol), F32)
    return pl.pallas_call(
        functools.partial(_attn_sample_kernel, lam_init=lam_init, past=past),
        grid=(bsz, nh // heads, past // tkc),
        in_specs=[new, old, old, new, new, new, vec(HC_DIM), vec(HC_DIM), vec(HC_DIM), vec(HC_DIM), vec(LANES)],
        out_specs=new,
        out_shape=jax.ShapeDtypeStruct((bsz, tlen, hd), F32),
        scratch_shapes=[stat, stat, pltpu.VMEM((LANES, ncol), F32)],
        compiler_params=_params(("parallel", "parallel", "arbitrary")),
        name="attn_sample",
    )(q3, ck5, cv5, kn3, vn3, z3, lq1, lk1, lq2, lk2, sg)


def _row_tile(n, pref):
    while n % pref:
        pref //= 2
    return pref


def _even_layer(x3, conv_buf, s0, w_in, w_out, cw, cb, cg, cbb, lb_logits, hn_g, pg, pb, layer, alpha):
    bsz, tlen, d = x3.shape
    n = bsz * tlen
    x2 = x3.reshape(n, d)
    d_a = cw.shape[-1]
    (h,) = _linear(x2, w_in, 1, _row_tile(n, 1024), 1024)
    h3 = h.reshape(bsz, tlen, -1)
    buf = jnp.pad(conv_buf, ((0, 0), (HIST - (CONV_W - 1), 0), (0, 0)))
    out_a, nbuf = _conv_branch(h3, buf, cw, cb, cg, cbb, _row_tile(tlen, 256))
    chunk = CHUNK if tlen % CHUNK == 0 else tlen
    tc = _row_tile(tlen, 512) if tlen % CHUNK == 0 else chunk
    d_b = s0.shape[1] * s0.shape[2]
    out_b, s_new = _hgrn_branch(h3, lb_logits, hn_g, s0, layer, 3 * d_a // d_b, tc, HGRN_HEADS_PER_STEP)
    xn = _out_ln(out_a.reshape(n, -1), 0, out_b.reshape(n, -1), 0, w_out, x2, pg, pb, alpha, _row_tile(n, 512))
    return xn.reshape(bsz, tlen, d), nbuf[:, HIST - (CONV_W - 1):], s_new


def _odd_layer(x3, caches, w_in, w_out, lq1, lk1, lq2, lk2, sg, pg, pb, lam_init, alpha):
    bsz, tlen, d = x3.shape
    n = bsz * tlen
    x2 = x3.reshape(n, d)
    d_c = w_in.shape[1] // 4
    tm = _row_tile(n, 1024)
    q, k, v, z = [y.reshape(bsz, tlen, d_c) for y in _linear(x2, w_in, 4, tm, 1024)]
    if caches is None:
        tk = _row_tile(tlen, 512)
        o = _attn_prompt(q, k, v, z, lq1, lk1, lq2, lk2, sg, lam_init, tk, tk)
    else:
        ck, cv, layer = caches
        o = _attn_sample(q, ck, cv, layer, k, v, z, lq1, lk1, lq2, lk2, sg, lam_init,
                         _row_tile(ck.shape[2], 1024), SUBLANES)
    o2 = o.reshape(n, d_c)
    xn = _out_ln(o2, 0, o2, 1, w_out, x2, pg, pb, alpha, _row_tile(n, 512))
    nh = d_c // LANES
    return xn.reshape(bsz, tlen, d), k.reshape(bsz, tlen, nh, LANES), v.reshape(bsz, tlen, nh, LANES)


def kernel(x_prompt, x_sample, state_conv, state_hgrn, cache_k, cache_v, w_in_even, w_out_even, conv_w, conv_b,
           conv_ln_g, conv_ln_b, hgrn_lb_logits, hgrn_norm_g, w_in_odd, w_out_odd, lam_q1, lam_k1, lam_q2, lam_k2,
           subln_g, post_ln_g, post_ln_b):
    depth = post_ln_g.shape[0]
    alpha = (2.0 * depth) ** 0.25
    xp, xs = x_prompt, x_sample
    bp = xp.shape[0]
    row = lambda a: a.reshape(1, -1)
    outs = {name: [] for name in ("conv_p", "hgrn_p", "k_p", "v_p", "conv_s", "hgrn_s", "k_s", "v_s")}
    for l in range(depth):
        pg, pb = row(post_ln_g[l]), row(post_ln_b[l])
        if l % 2 == 0:
            e = l // 2
            prm = (w_in_even[e].astype(BF16), w_out_even[e].astype(BF16), conv_w[e], row(conv_b[e]),
                   row(conv_ln_g[e]), row(conv_ln_b[e]), hgrn_lb_logits, row(hgrn_norm_g[e]), pg, pb, e, alpha)
            buf0 = jnp.zeros((bp,) + state_conv.shape[2:], F32)
            s0 = jnp.zeros((bp,) + state_hgrn.shape[2:], F32)
            xp, cb_p, s_p = _even_layer(xp, buf0, s0, *prm)
            xs, cb_s, s_s = _even_layer(xs, state_conv[e], state_hgrn[e], *prm)
            outs["conv_p"].append(cb_p); outs["hgrn_p"].append(s_p)
            outs["conv_s"].append(cb_s); outs["hgrn_s"].append(s_s)
        else:
            o = l // 2
            lam_init = 0.8 - 0.6 * math.exp(-0.3 * l)
            prm = (w_in_odd[o].astype(BF16), w_out_odd[o].astype(BF16), row(lam_q1[o]), row(lam_k1[o]),
                   row(lam_q2[o]), row(lam_k2[o]), row(subln_g[o]), pg, pb, lam_init, alpha)
            xp, kn_p, vn_p = _odd_layer(xp, None, *prm)
            xs, kn_s, vn_s = _odd_layer(xs, (cache_k, cache_v, o), *prm)
            outs["k_p"].append(kn_p); outs["v_p"].append(vn_p)
            outs["k_s"].append(kn_s); outs["v_s"].append(vn_s)
    st = lambda name: jnp.stack(outs[name])
    return (xp, xs, st("conv_p"), st("hgrn_p"), st("k_p"), st("v_p"),
            st("conv_s"), st("hgrn_s"), st("k_s"), st("v_s"))
```

```python
import functools
import math

import jax
import jax.numpy as jnp
from jax import lax
from jax.experimental import pallas as pl
from jax.experimental.pallas import tpu as pltpu

F32 = jnp.float32
BF16 = jnp.bfloat16
MIXER_OUT_DTYPE = BF16

CHUNK = 64
SUBLANES = 8
CONV_W = 31
HIST = 32
HB_DK = 128
HC_DIM = 64
LANES = 128
EPS = 1e-5
LOG2E = math.log2(math.e)
SAMPLE_ROW_PIECES = 4
HGRN_HEADS_PER_STEP = 8
VMEM_LIMIT = 48 * 1024 * 1024


def _params(sem):
    return pltpu.CompilerParams(dimension_semantics=sem, vmem_limit_bytes=VMEM_LIMIT)


def _dot(a, b):
    return jnp.dot(a, b, preferred_element_type=F32)


def _dot_nt(a, b):
    return lax.dot_general(a, b, (((1,), (1,)), ((), ())), preferred_element_type=F32)


def _dot_tn(a, b):
    return lax.dot_general(a, b, (((0,), (0,)), ((), ())), preferred_element_type=F32)


def _sigmoid(x):
    return 1.0 / (1.0 + jnp.exp(-x))


def _silu(x):
    return x * _sigmoid(x)


def _linear_kernel(x_ref, w_ref, *o_refs, blocks_per_out):
    j = pl.program_id(1)
    if len(o_refs) == 1:
        o_refs[0][...] = _dot(x_ref[...].astype(BF16), w_ref[...])
        return
    for k, o_ref in enumerate(o_refs):
        @pl.when((j >= k * blocks_per_out) & (j < (k + 1) * blocks_per_out))
        def _(o_ref=o_ref):
            o_ref[...] = _dot(x_ref[...].astype(BF16), w_ref[...])


def _linear(x, w, nout, tm, tn):
    n, k = x.shape
    width = w.shape[1] // nout
    assert n % tm == 0 and width % tn == 0 and width * nout == w.shape[1]
    bpo = width // tn

    def out_map(g):
        return lambda i, j: (i, jnp.clip(j - g * bpo, 0, bpo - 1))

    return pl.pallas_call(
        functools.partial(_linear_kernel, blocks_per_out=bpo),
        grid=(n // tm, nout * bpo),
        in_specs=[pl.BlockSpec((tm, k), lambda i, j: (i, 0)),
                  pl.BlockSpec((k, tn), lambda i, j: (0, j))],
        out_specs=[pl.BlockSpec((tm, tn), out_map(g)) for g in range(nout)],
        out_shape=[jax.ShapeDtypeStruct((n, width), F32) for _ in range(nout)],
        compiler_params=_params(("parallel", "arbitrary")),
        name="linear",
    )(x, w)


def _conv_kernel(a_ref, ga_ref, za_ref, buf_ref, w_ref, cb_ref, g_ref, b_ref, out_ref, nb_ref, scr, c_scr, sh_scr,
                 *, tt, rb):
    t = pl.program_id(1)
    nch = a_ref.shape[-1] // LANES

    @pl.when(t == 0)
    def _():
        scr[0:HIST, :] = buf_ref[...]

    @pl.when(t > 0)
    def _():
        scr[0:HIST, :] = scr[tt:tt + HIST, :]

    scr[HIST:HIST + tt, :] = a_ref[...] * _sigmoid(ga_ref[...])

    def row_block(r, carry):
        r0 = pl.multiple_of(r * rb, rb)
        for c in range(nch):
            cs = slice(c * LANES, (c + 1) * LANES)
            slab = scr[pl.ds(r0, rb + HIST), cs]
            for shift in range(SUBLANES):
                n = (rb + HIST - shift) // SUBLANES * SUBLANES
                sh_scr[shift, 0:n, :] = slab[shift:shift + n, :]
            acc = jnp.zeros((rb, LANES), F32)
            for j in range(CONV_W):
                shift = (j + 2) % SUBLANES
                off = j + 2 - shift
                acc = acc + w_ref[j:j + 1, cs] * sh_scr[shift, off:off + rb, :]
            c_scr[pl.ds(r0, rb), cs] = acc + cb_ref[:, cs]
        return carry

    lax.fori_loop(0, tt // rb, row_block, 0)

    cv = c_scr[...]
    mu = jnp.mean(cv, axis=-1, keepdims=True)
    xc = cv - mu
    var = jnp.mean(xc * xc, axis=-1, keepdims=True)
    y = xc * lax.rsqrt(var + EPS) * g_ref[...] + b_ref[...]
    out_ref[...] = (_silu(y) * _silu(za_ref[...])).astype(out_ref.dtype)

    @pl.when(t == pl.num_programs(1) - 1)
    def _():
        nb_ref[...] = scr[tt:tt + HIST, :]


def _conv_branch(h3, buf, cw, cb, g, b, tt):
    bsz, tlen, _ = h3.shape
    c = cw.shape[-1]
    assert tlen % tt == 0
    rb = min(64, tt)
    assert tt % rb == 0
    row = lambda col: pl.BlockSpec((None, tt, c), lambda bi, ti: (bi, ti, col))
    vec = lambda r: pl.BlockSpec((r, c), lambda bi, ti: (0, 0))
    hist = pl.BlockSpec((None, HIST, c), lambda bi, ti: (bi, 0, 0))
    return pl.pallas_call(
        functools.partial(_conv_kernel, tt=tt, rb=rb),
        grid=(bsz, tlen // tt),
        in_specs=[row(0), row(1), row(2), hist, vec(CONV_W), vec(1), vec(1), vec(1)],
        out_specs=[pl.BlockSpec((None, tt, c), lambda bi, ti: (bi, ti, 0)), hist],
        out_shape=[jax.ShapeDtypeStruct((bsz, tlen, c), MIXER_OUT_DTYPE), jax.ShapeDtypeStruct((bsz, HIST, c), F32)],
        scratch_shapes=[pltpu.VMEM((tt + HIST, c), F32), pltpu.VMEM((tt, c), F32),
                        pltpu.VMEM((SUBLANES, rb + HIST, LANES), F32)],
        compiler_params=_params(("parallel", "arbitrary")),
        name="conv_branch",
    )(h3, h3, h3, buf, cw, cb, g, b)


def _cumsum_rows(tri3, g):
    g0 = g.astype(BF16)
    r1 = g - g0.astype(F32)
    g1 = r1.astype(BF16)
    g2 = (r1 - g1.astype(F32)).astype(BF16)
    return _dot(tri3, jnp.concatenate([g0, g1, g2], axis=0))


def _pair_reference(x, h):
    nrows, width = x.shape
    if h >= SUBLANES:
        parts = [jnp.broadcast_to(x[p + h - 1:p + h], (2 * h, width)) for p in range(0, nrows, 2 * h)]
        return parts[0] if len(parts) == 1 else jnp.concatenate(parts, axis=0)
    x3 = x.reshape(nrows // SUBLANES, SUBLANES, width)
    sub = lax.broadcasted_iota(jnp.int32, (1, SUBLANES, 1), 1)
    ref = None
    for p in range(SUBLANES - 2 * h, -1, -2 * h):
        row = x3[:, p + h - 1:p + h, :]
        ref = row if ref is None else jnp.where(sub < p + 2 * h, row, ref)
    return jnp.broadcast_to(ref, x3.shape).reshape(nrows, width)


def _hgrn_kernel(q_ref, f_ref, i_ref, zb_ref, lbl_ref, hg_ref, s0_ref, o_ref, s_ref, st_scr,
                 *, layer, chunk, nchunks, heads):
    t = pl.program_id(2)
    dk = HB_DK

    @pl.when(t == 0)
    def _():
        for h in range(heads):
            st_scr[h] = s0_ref[h].T

    lg = lbl_ref[...]
    ex = jnp.exp(lg - jnp.max(lg, axis=0, keepdims=True))
    sm = ex / jnp.sum(ex, axis=0, keepdims=True)
    lb = jnp.zeros_like(sm[0:1])
    for r in range(1, layer + 1):
        lb = lb + sm[r:r + 1]
    log_lb = jnp.log(lb)
    log1m_lb = jnp.log1p(-lb)

    rows = lax.broadcasted_iota(jnp.int32, (chunk, chunk), 0)
    cols = lax.broadcasted_iota(jnp.int32, (chunk, chunk), 1)
    tri = jnp.where(rows >= cols, 1.0, 0.0).astype(BF16)
    tri3 = jnp.concatenate([tri, tri, tri], axis=1)
    rows2 = lax.broadcasted_iota(jnp.int32, (chunk, 2 * chunk), 0)
    cols2 = lax.rem(lax.broadcasted_iota(jnp.int32, (chunk, 2 * chunk), 1), chunk)
    xor2 = jnp.bitwise_xor(rows2, cols2)
    levels = [0]
    h = 1
    while h < chunk:
        levels.append(h)
        h *= 2
    heads_cols = [slice(hd * dk, (hd + 1) * dk) for hd in range(heads)]
    zero_blk = jnp.zeros((chunk, dk), BF16)

    def block_diag(x, g):
        a, b = x[:, heads_cols[2 * g]], x[:, heads_cols[2 * g + 1]]
        return jnp.concatenate([jnp.concatenate([a, zero_blk], axis=1), jnp.concatenate([zero_blk, b], axis=1)],
                               axis=0)

    def one_chunk(ci, carry):
        r0 = pl.multiple_of(ci * chunk, chunk)
        rs = pl.ds(r0, chunk)
        fr = f_ref[rs, :]
        lsig = jnp.minimum(fr, 0.0) - jnp.log(1.0 + jnp.exp(-jnp.abs(fr)))
        c2 = log1m_lb + lsig
        logf = jnp.maximum(log_lb, c2) + jnp.log(1.0 + jnp.exp(-jnp.abs(log_lb - c2)))
        kk = (1.0 - lb) * _sigmoid(-fr)
        qq = _silu(q_ref[rs, :])
        vv = i_ref[rs, :]
        vb = vv.astype(BF16)
        bl = _cumsum_rows(tri3, logf) * LOG2E
        qe = (qq * jnp.exp2(bl)).astype(BF16)
        o = jnp.concatenate([_dot_nt(qe[:, cs], st_scr[hd].astype(BF16)) for hd, cs in enumerate(heads_cols)],
                            axis=1)
        att = [jnp.zeros((chunk, 2 * chunk), F32) for _ in range(heads // 2)]
        for h in levels:
            if h == 0:
                qf, kf, keep = qq.astype(BF16), kk.astype(BF16), rows2 == cols2
            else:
                fac = jnp.exp2(-jnp.abs(bl - _pair_reference(bl, h)))
                qf, kf = (qq * fac).astype(BF16), (kk * fac).astype(BF16)
                keep = (xor2 >= h) & (xor2 < 2 * h) & (rows2 > cols2)
            for g in range(heads // 2):
                a_h = _dot_nt(qf[:, 2 * g * dk:(2 * g + 2) * dk], block_diag(kf, g))
                att[g] = jnp.where(keep, a_h, att[g])
        o = o + jnp.concatenate([_dot(att[g].astype(BF16), block_diag(vb, g)) for g in range(heads // 2)], axis=1)
        bl_last = bl[chunk - 1:chunk]
        kd = (kk * jnp.exp2(bl_last - bl)).astype(BF16)
        decay = jnp.exp2(bl_last)
        for hd, cs in enumerate(heads_cols):
            st_scr[hd] = st_scr[hd] * decay[:, cs] + _dot_tn(vb[:, cs], kd[:, cs])
        inv = jnp.concatenate(
            [jnp.broadcast_to(lax.rsqrt(jnp.mean(o[:, cs] * o[:, cs], axis=-1, keepdims=True) + EPS), (chunk, dk))
             for cs in heads_cols], axis=1)
        o_ref[rs, :] = (o * inv * jnp.concatenate([hg_ref[...]] * heads, axis=1)
                        * _silu(zb_ref[rs, :])).astype(o_ref.dtype)
        return carry

    lax.fori_loop(0, nchunks, one_chunk, 0)

    @pl.when(t == pl.num_programs(2) - 1)
    def _():
        for h in range(heads):
            s_ref[h] = st_scr[h].T


def _hgrn_branch(h3, lb_logits, hn_g, s0, layer, col_q, tc, heads):
    bsz, tlen, _ = h3.shape
    nh, dk, dv = s0.shape[1:]
    d_b = nh * dk
    chunk = CHUNK if tlen % CHUNK == 0 else tlen
    assert chunk % SUBLANES == 0 and chunk & (chunk - 1) == 0
    assert tc % chunk == 0 and tlen % tc == 0 and nh % heads == 0
    w = heads * dk
    gpb = d_b // w
    col = lambda k: pl.BlockSpec((None, tc, w), lambda bi, hi, ti: (bi, ti, (col_q + k) * gpb + hi))
    st = pl.BlockSpec((None, heads, dk, dv), lambda bi, hi, ti: (bi, hi, 0, 0))
    ne = lb_logits.shape[0]
    return pl.pallas_call(
        functools.partial(_hgrn_kernel, layer=layer, chunk=chunk, nchunks=tc // chunk, heads=heads),
        grid=(bsz, nh // heads, tlen // tc),
        in_specs=[col(0), col(1), col(2), col(3),
                  pl.BlockSpec((ne, w), lambda bi, hi, ti: (0, hi)),
                  pl.BlockSpec((1, dv), lambda bi, hi, ti: (0, 0)),
                  st],
        out_specs=[pl.BlockSpec((None, tc, w), lambda bi, hi, ti: (bi, ti, hi)), st],
        out_shape=[jax.ShapeDtypeStruct((bsz, tlen, d_b), MIXER_OUT_DTYPE), jax.ShapeDtypeStruct(s0.shape, F32)],
        scratch_shapes=[pltpu.VMEM((heads, dv, dk), F32)],
        compiler_params=_params(("parallel", "parallel", "arbitrary")),
        name="hgrn_branch",
    )(h3, h3, h3, h3, lb_logits, hn_g, s0)


def _out_ln_kernel(a_ref, b_ref, w_ref, x_ref, g_ref, bb_ref, o_ref, *, alpha):
    ka = a_ref.shape[-1]
    tm = x_ref.shape[0]
    halves = [pl.ds(i * (tm // 2), tm // 2) for i in range(2)] if tm % 16 == 0 else [pl.ds(0, tm)]
    ys = [_dot(a_ref[rs, :].astype(BF16), w_ref[0:ka, :]) + _dot(b_ref[rs, :].astype(BF16), w_ref[ka:, :])
          for rs in halves]
    for rs, y in zip(halves, ys):
        r = alpha * x_ref[rs, :] + y
        mu = jnp.mean(r, axis=-1, keepdims=True)
        rc = r - mu
        var = jnp.mean(rc * rc, axis=-1, keepdims=True)
        o_ref[rs, :] = rc * lax.rsqrt(var + EPS) * g_ref[...] + bb_ref[...]


def _out_ln(a, a_col, b, b_col, w, x, g, bb, alpha, tm):
    n, d = x.shape
    kh = w.shape[0] // 2
    assert n % tm == 0
    return pl.pallas_call(
        functools.partial(_out_ln_kernel, alpha=alpha),
        grid=(n // tm,),
        in_specs=[pl.BlockSpec((tm, kh), lambda i: (i, a_col)),
                  pl.BlockSpec((tm, kh), lambda i: (i, b_col)),
                  pl.BlockSpec(w.shape, lambda i: (0, 0)),
                  pl.BlockSpec((tm, d), lambda i: (i, 0)),
                  pl.BlockSpec((1, d), lambda i: (0, 0)),
                  pl.BlockSpec((1, d), lambda i: (0, 0))],
        out_specs=pl.BlockSpec((tm, d), lambda i: (i, 0)),
        out_shape=jax.ShapeDtypeStruct((n, d), F32),
        compiler_params=_params(("parallel",)),
        name="out_ln",
    )(a, b, w, x, g, bb)


def _lambda(lq1, lk1, lq2, lk2, lam_init):
    e1 = jnp.exp(jnp.sum(lq1[...] * lk1[...], axis=-1, keepdims=True))
    e2 = jnp.exp(jnp.sum(lq2[...] * lk2[...], axis=-1, keepdims=True))
    return e1 - e2 + lam_init


def _map_queries(q, scale):
    q = q * scale
    lane = lax.broadcasted_iota(jnp.int32, q.shape, 1)
    return jnp.concatenate([jnp.where(lane < HC_DIM, q, 0.0), jnp.where(lane >= HC_DIM, q, 0.0)],
                           axis=0).astype(BF16)


def _attn_finish(o1, o2, lam, sg_ref, z, o_ref, cols, lam_init):
    o = o1 - lam * o2
    o = o * lax.rsqrt(jnp.mean(o * o, axis=-1, keepdims=True) + EPS) * sg_ref[...] * (1.0 - lam_init)
    o_ref[:, cols] = (o * _silu(z)).astype(o_ref.dtype)


def _attn_prompt_kernel(q_ref, k_ref, v_ref, z_ref, lq1, lk1, lq2, lk2, sg_ref, o_ref,
                        kb_scr, vb_scr, m_scr, acc_scr, s_scr, bias_scr, *, tq, tk, lam_init):
    qi = pl.program_id(2)
    tlen = k_ref.shape[0]
    nlt = tk // LANES

    @pl.when(qi == 0)
    def _():
        kb_scr[...] = k_ref[...].astype(BF16)
        vb_scr[:, 0:LANES] = v_ref[...].astype(BF16)
        vb_scr[:, LANES:] = jnp.ones((tlen, LANES), BF16)
        q_chunk = lax.div(lax.broadcasted_iota(jnp.int32, (tq, 1), 0), CHUNK)
        k_chunk = lax.div(lax.broadcasted_iota(jnp.int32, (1, tk), 1), CHUNK)
        bias_scr[...] = jnp.where(k_chunk <= q_chunk, 0.0, -jnp.inf)

    q2 = _map_queries(q_ref[...], (HC_DIM ** -0.5) * LOG2E)
    m_scr[...] = jnp.full(m_scr.shape, -jnp.inf, F32)
    acc_scr[...] = jnp.zeros(acc_scr.shape, F32)

    def scores(kj):
        k0 = pl.multiple_of(kj * tk, tk)
        return _dot_nt(q2, kb_scr[pl.ds(k0, tk), :])

    def consume(s, rows, k_start, nkeys):
        tiles = [s[:, t * LANES:(t + 1) * LANES] for t in range(nkeys // LANES)]
        m_cur = tiles[0]
        for tl in tiles[1:]:
            m_cur = jnp.maximum(m_cur, tl)
        m_old = m_scr[rows, :]
        m_new = jnp.maximum(m_old, jnp.max(m_cur, axis=-1, keepdims=True))
        alpha = jnp.exp2(m_old - m_new)
        p = jnp.concatenate([jnp.exp2(tl - m_new).astype(BF16) for tl in tiles], axis=1)
        acc_scr[rows, :] = (jnp.concatenate([alpha, alpha], axis=1) * acc_scr[rows, :]
                            + _dot(p, vb_scr[pl.ds(k_start, nkeys), :]))
        m_scr[rows, :] = m_new

    def block(kj, prefetch):
        s = s_scr[...]
        if prefetch is not None:
            s_scr[...] = scores(prefetch)
        consume(s, slice(None), pl.multiple_of(kj * tk, tk), tk)

    def diagonal_block():
        half = tq // 2
        k_lo = pl.multiple_of(qi * tk, tk)
        k_hi = pl.multiple_of(qi * tk + half, half)
        bias = bias_scr[...]
        s_lo = _dot_nt(q2, kb_scr[pl.ds(k_lo, half), :]) + jnp.concatenate([bias[:, 0:half]] * 2, axis=0)
        q_hi = jnp.concatenate([q2[half:tq], q2[tq + half:]], axis=0)
        s_hi = _dot_nt(q_hi, kb_scr[pl.ds(k_hi, half), :]) + jnp.concatenate([bias[half:, half:]] * 2, axis=0)
        s_scr[...] = scores(0)
        consume(s_lo, slice(None), k_lo, half)
        consume(s_hi[0:half], slice(half, tq), k_hi, half)
        consume(s_hi[half:], slice(tq + half, 2 * tq), k_hi, half)

    def two_blocks(i, carry):
        block(2 * i, 2 * i + 1)
        block(2 * i + 1, 2 * i + 2)
        return carry

    diagonal_block()
    npre = qi - 1
    lax.fori_loop(0, npre // 2, two_blocks, 0)

    @pl.when((npre > 0) & (npre % 2 == 1))
    def _():
        block(npre - 1, npre)

    @pl.when(qi > 0)
    def _():
        block(qi - 1, None)

    lam = _lambda(lq1, lk1, lq2, lk2, lam_init)
    acc = acc_scr[...]
    _attn_finish(acc[0:tq, 0:LANES] / acc[0:tq, LANES:], acc[tq:, 0:LANES] / acc[tq:, LANES:],
                 lam, sg_ref, z_ref[...], o_ref, slice(None), lam_init)


def _attn_prompt(q3, k3, v3, z3, lq1, lk1, lq2, lk2, sg, lam_init, tq, tk):
    bsz, tlen, hd = q3.shape
    nh = hd // LANES
    assert tlen % tk == 0 and tk == tq and tq % CHUNK == 0
    qblk = pl.BlockSpec((None, tq, LANES), lambda bi, hi, qi: (bi, qi, hi))
    kblk = pl.BlockSpec((None, tlen, LANES), lambda bi, hi, qi: (bi, 0, hi))
    vec = lambda n: pl.BlockSpec((1, n), lambda bi, hi, qi: (0, 0))
    return pl.pallas_call(
        functools.partial(_attn_prompt_kernel, tq=tq, tk=tk, lam_init=lam_init),
        grid=(bsz, nh, tlen // tq),
        in_specs=[qblk, kblk, kblk, qblk, vec(HC_DIM), vec(HC_DIM), vec(HC_DIM), vec(HC_DIM), vec(LANES)],
        out_specs=qblk,
        out_shape=jax.ShapeDtypeStruct((bsz, tlen, hd), MIXER_OUT_DTYPE),
        scratch_shapes=[pltpu.VMEM((tlen, LANES), BF16), pltpu.VMEM((tlen, 2 * LANES), BF16),
                        pltpu.VMEM((2 * tq, LANES), F32), pltpu.VMEM((2 * tq, 2 * LANES), F32),
                        pltpu.VMEM((2 * tq, tk), F32), pltpu.VMEM((tq, tk), F32)],
        compiler_params=_params(("parallel", "parallel", "arbitrary")),
        name="attn_prompt",
    )(q3, k3, v3, z3, lq1, lk1, lq2, lk2, sg)


def _attn_sample_kernel(q_ref, ck_ref, cv_ref, kn_ref, vn_ref, z_ref, lq1, lk1, lq2, lk2, sg_ref, o_ref,
                        m_scr, l_scr, acc_scr, *, lam_init, past):
    kb = pl.program_id(2)
    tlen = q_ref.shape[0]
    tkc, heads = ck_ref.shape[0], ck_ref.shape[1]
    per_head = 2 * tlen
    ncol = heads * per_head

    @pl.when(kb == 0)
    def _():
        m_scr[...] = jnp.full(m_scr.shape, -jnp.inf, F32)
        l_scr[...] = jnp.zeros(l_scr.shape, F32)
        acc_scr[...] = jnp.zeros(acc_scr.shape, F32)

    head_cols = [slice(h * LANES, (h + 1) * LANES) for h in range(heads)]
    scale = (HC_DIM ** -0.5) * LOG2E
    qw = jnp.concatenate([_map_queries(q_ref[:, cs], scale) for cs in head_cols], axis=0)
    col = lax.broadcasted_iota(jnp.int32, (1, ncol), 1)
    col_head = lax.div(col, per_head)
    q_chunk = lax.div(past + lax.rem(col, tlen), CHUNK)

    def update(k_ref, v_ref, row_head, row_pos):
        k, v = k_ref().astype(BF16), v_ref().astype(BF16)
        n, cyc = k.shape[0], row_head.shape[0]
        own = jnp.where(row_head == col_head, 0.0, -jnp.inf)
        pieces = SAMPLE_ROW_PIECES if n % (SAMPLE_ROW_PIECES * cyc) == 0 else 1
        step = n // pieces
        scores = lambda i: _dot_nt(k[i * step:(i + 1) * step], qw)
        s_next = scores(0)
        for i in range(pieces):
            s = s_next
            if i + 1 < pieces:
                s_next = scores(i + 1)
            s = (s.reshape(step // cyc, cyc, ncol) + own[None]).reshape(step, ncol)
            if row_pos is not None:
                s = jnp.where(lax.div(row_pos[i * step:(i + 1) * step], CHUNK) <= q_chunk, s, -jnp.inf)
            m_old = m_scr[...]
            m_new = jnp.maximum(m_old, jnp.max(s, axis=0, keepdims=True))
            alpha = jnp.exp2(m_old - m_new)
            p = jnp.exp2(s - m_new)
            l_scr[...] = alpha * l_scr[...] + jnp.sum(p, axis=0, keepdims=True)
            acc_scr[...] = alpha * acc_scr[...] + _dot_tn(v[i * step:(i + 1) * step], p.astype(BF16))
            m_scr[...] = m_new

    cached_k = lambda: ck_ref[...].reshape(tkc * heads, LANES)
    cached_v = lambda: cv_ref[...].reshape(tkc * heads, LANES)
    cached_head = lax.broadcasted_iota(jnp.int32, (heads, 1), 0)
    all_visible = (kb * tkc + tkc - 1) // CHUNK <= past // CHUNK

    @pl.when(all_visible)
    def _():
        update(cached_k, cached_v, cached_head, None)

    @pl.when(jnp.logical_not(all_visible))
    def _():
        row = lax.broadcasted_iota(jnp.int32, (tkc * heads, 1), 0)
        update(cached_k, cached_v, cached_head, kb * tkc + lax.div(row, heads))

    @pl.when(kb == pl.num_programs(2) - 1)
    def _():
        row_n = lax.broadcasted_iota(jnp.int32, (tlen * heads, 1), 0)
        update(lambda: jnp.concatenate([kn_ref[:, cs] for cs in head_cols], axis=0),
               lambda: jnp.concatenate([vn_ref[:, cs] for cs in head_cols], axis=0),
               lax.div(row_n, tlen), past + lax.rem(row_n, tlen))
        o = (acc_scr[...] / l_scr[...]).T
        lam = _lambda(lq1, lk1, lq2, lk2, lam_init)
        for h, cs in enumerate(head_cols):
            o_h = o[h * per_head:(h + 1) * per_head]
            _attn_finish(o_h[0:tlen], o_h[tlen:], lam, sg_ref, z_ref[:, cs], o_ref, cs, lam_init)


def _attn_sample(q3, ck5, cv5, layer, kn3, vn3, z3, lq1, lk1, lq2, lk2, sg, lam_init, tkc, heads):
    bsz, tlen, hd = q3.shape
    past, nh = ck5.shape[2], ck5.shape[3]
    assert past % tkc == 0 and nh % heads == 0
    ncol = heads * 2 * tlen
    new = pl.BlockSpec((None, tlen, heads * LANES), lambda bi, hi, ki: (bi, 0, hi))
    old = pl.BlockSpec((None, None, tkc, heads, LANES), lambda bi, hi, ki: (layer, bi, ki, hi, 0))
    vec = lambda n: pl.BlockSpec((1, n), lambda bi, hi, ki: (0, 0))
    stat = pltpu.VMEM((1, ncol), F32)
    return pl.pallas_call(
        functools.partial(_attn_sample_kernel, lam_init=lam_init, past=past),
        grid=(bsz, nh // heads, past // tkc),
        in_specs=[new, old, old, new, new, new, vec(HC_DIM), vec(HC_DIM), vec(HC_DIM), vec(HC_DIM), vec(LANES)],
        out_specs=new,
        out_shape=jax.ShapeDtypeStruct((bsz, tlen, hd), MIXER_OUT_DTYPE),
        scratch_shapes=[stat, stat, pltpu.VMEM((LANES, ncol), F32)],
        compiler_params=_params(("parallel", "parallel", "arbitrary")),
        name="attn_sample",
    )(q3, ck5, cv5, kn3, vn3, z3, lq1, lk1, lq2, lk2, sg)


def _row_tile(n, pref):
    while n % pref:
        pref //= 2
    return pref


def _even_layer(x3, conv_buf, s0, w_in, w_out, cw, cb, cg, cbb, lb_logits, hn_g, pg, pb, layer, alpha):
    bsz, tlen, d = x3.shape
    n = bsz * tlen
    x2 = x3.reshape(n, d)
    d_a = cw.shape[-1]
    (h,) = _linear(x2, w_in, 1, _row_tile(n, 1024), w_in.shape[1] // 4)
    h3 = h.reshape(bsz, tlen, -1)
    buf = jnp.pad(conv_buf, ((0, 0), (HIST - (CONV_W - 1), 0), (0, 0)))
    out_a, nbuf = _conv_branch(h3, buf, cw, cb, cg, cbb, _row_tile(tlen, 256))
    chunk = CHUNK if tlen % CHUNK == 0 else tlen
    tc = _row_tile(tlen, 512) if tlen % CHUNK == 0 else chunk
    d_b = s0.shape[1] * s0.shape[2]
    out_b, s_new = _hgrn_branch(h3, lb_logits, hn_g, s0, layer, 3 * d_a // d_b, tc, HGRN_HEADS_PER_STEP)
    xn = _out_ln(out_a.reshape(n, -1), 0, out_b.reshape(n, -1), 0, w_out, x2, pg, pb, alpha, _row_tile(n, 512))
    return xn.reshape(bsz, tlen, d), nbuf[:, HIST - (CONV_W - 1):], s_new


def _odd_layer(x3, caches, w_in, w_out, lq1, lk1, lq2, lk2, sg, pg, pb, lam_init, alpha):
    bsz, tlen, d = x3.shape
    n = bsz * tlen
    x2 = x3.reshape(n, d)
    d_c = w_in.shape[1] // 4
    tm = _row_tile(n, 1024)
    q, k, v, z = [y.reshape(bsz, tlen, d_c) for y in _linear(x2, w_in, 4, tm, 1024)]
    if caches is None:
        tk = _row_tile(tlen, 512)
        o = _attn_prompt(q, k, v, z, lq1, lk1, lq2, lk2, sg, lam_init, tk, tk)
    else:
        ck, cv, layer = caches
        o = _attn_sample(q, ck, cv, layer, k, v, z, lq1, lk1, lq2, lk2, sg, lam_init,
                         _row_tile(ck.shape[2], 1024), SUBLANES)
    o2 = o.reshape(n, d_c)
    xn = _out_ln(o2, 0, o2, 1, w_out, x2, pg, pb, alpha, _row_tile(n, 512))
    nh = d_c // LANES
    return xn.reshape(bsz, tlen, d), k.reshape(bsz, tlen, nh, LANES), v.reshape(bsz, tlen, nh, LANES)


def kernel(x_prompt, x_sample, state_conv, state_hgrn, cache_k, cache_v, w_in_even, w_out_even, conv_w, conv_b,
           conv_ln_g, conv_ln_b, hgrn_lb_logits, hgrn_norm_g, w_in_odd, w_out_odd, lam_q1, lam_k1, lam_q2, lam_k2,
           subln_g, post_ln_g, post_ln_b):
    depth = post_ln_g.shape[0]
    alpha = (2.0 * depth) ** 0.25
    xp, xs = x_prompt, x_sample
    bp = xp.shape[0]
    row = lambda a: a.reshape(1, -1)
    outs = {name: [] for name in ("conv_p", "hgrn_p", "k_p", "v_p", "conv_s", "hgrn_s", "k_s", "v_s")}
    for l in range(depth):
        pg, pb = row(post_ln_g[l]), row(post_ln_b[l])
        if l % 2 == 0:
            e = l // 2
            prm = (w_in_even[e].astype(BF16), w_out_even[e].astype(BF16), conv_w[e], row(conv_b[e]),
                   row(conv_ln_g[e]), row(conv_ln_b[e]), hgrn_lb_logits, row(hgrn_norm_g[e]), pg, pb, e, alpha)
            buf0 = jnp.zeros((bp,) + state_conv.shape[2:], F32)
            s0 = jnp.zeros((bp,) + state_hgrn.shape[2:], F32)
            xp, cb_p, s_p = _even_layer(xp, buf0, s0, *prm)
            xs, cb_s, s_s = _even_layer(xs, state_conv[e], state_hgrn[e], *prm)
            outs["conv_p"].append(cb_p); outs["hgrn_p"].append(s_p)
            outs["conv_s"].append(cb_s); outs["hgrn_s"].append(s_s)
        else:
            o = l // 2
            lam_init = 0.8 - 0.6 * math.exp(-0.3 * l)
            prm = (w_in_odd[o].astype(BF16), w_out_odd[o].astype(BF16), row(lam_q1[o]), row(lam_k1[o]),
                   row(lam_q2[o]), row(lam_k2[o]), row(subln_g[o]), pg, pb, lam_init, alpha)
            xp, kn_p, vn_p = _odd_layer(xp, None, *prm)
            xs, kn_s, vn_s = _odd_layer(xs, (cache_k, cache_v, o), *prm)
            outs["k_p"].append(kn_p); outs["v_p"].append(vn_p)
            outs["k_s"].append(kn_s); outs["v_s"].append(vn_s)
    st = lambda name: jnp.stack(outs[name])
    return (xp, xs, st("conv_p"), st("hgrn_p"), st("k_p"), st("v_p"),
            st("conv_s"), st("hgrn_s"), st("k_s"), st("v_s"))
```

```python
import functools
import math

import jax
import jax.numpy as jnp
from jax import lax
from jax.experimental import pallas as pl
from jax.experimental.pallas import tpu as pltpu

F32 = jnp.float32
BF16 = jnp.bfloat16
MIXER_OUT_DTYPE = BF16

CHUNK = 64
SUBLANES = 8
LANES = 128
CONV_W = 31
HIST = 32
LEAD = HIST - (CONV_W - 1)
HB_DK = 128
HC_DIM = 64
EPS = 1e-5
LOG2E = math.log2(math.e)
VMEM_LIMIT = 48 * 1024 * 1024

LINEAR_ROWS = 1024
LINEAR_COLS = 1024
EVEN_PROJ_COL_TILES = 4
OUT_LN_ROWS = 512
CONV_TIME = 256
CONV_ROW_BLOCK = 64
HGRN_TIME = 512
HGRN_HEADS_PER_STEP = 8
ATTN_BLOCK = 512
DECODE_KEYS = 1024
DECODE_ROW_PIECES = 4


def _params(sem):
    return pltpu.CompilerParams(dimension_semantics=sem, vmem_limit_bytes=VMEM_LIMIT)


def _dot(a, b):
    return jnp.dot(a, b, preferred_element_type=F32)


def _dot_nt(a, b):
    return lax.dot_general(a, b, (((1,), (1,)), ((), ())), preferred_element_type=F32)


def _dot_tn(a, b):
    return lax.dot_general(a, b, (((0,), (0,)), ((), ())), preferred_element_type=F32)


def _sigmoid(x):
    return 1.0 / (1.0 + jnp.exp(-x))


def _silu(x):
    return x * _sigmoid(x)


def _linear_kernel(x_ref, w_ref, *o_refs, blocks_per_out):
    j = pl.program_id(1)
    if len(o_refs) == 1:
        o_refs[0][...] = _dot(x_ref[...].astype(BF16), w_ref[...])
        return
    for k, o_ref in enumerate(o_refs):
        @pl.when((j >= k * blocks_per_out) & (j < (k + 1) * blocks_per_out))
        def _(o_ref=o_ref):
            o_ref[...] = _dot(x_ref[...].astype(BF16), w_ref[...])


def _linear(x, w, nout, tm, tn):
    n, k = x.shape
    width = w.shape[1] // nout
    assert n % tm == 0 and width % tn == 0 and width * nout == w.shape[1]
    bpo = width // tn

    def out_map(g):
        return lambda i, j: (i, jnp.clip(j - g * bpo, 0, bpo - 1))

    return pl.pallas_call(
        functools.partial(_linear_kernel, blocks_per_out=bpo),
        grid=(n // tm, nout * bpo),
        in_specs=[pl.BlockSpec((tm, k), lambda i, j: (i, 0)),
                  pl.BlockSpec((k, tn), lambda i, j: (0, j))],
        out_specs=[pl.BlockSpec((tm, tn), out_map(g)) for g in range(nout)],
        out_shape=[jax.ShapeDtypeStruct((n, width), F32) for _ in range(nout)],
        compiler_params=_params(("parallel", "arbitrary")),
        name="linear",
    )(x, w)


def _conv_kernel(a_ref, ga_ref, za_ref, buf_ref, w_ref, cb_ref, g_ref, b_ref, out_ref, nb_ref, scr, c_scr, sh_scr,
                 *, tt, rb):
    t = pl.program_id(1)
    nch = a_ref.shape[-1] // LANES

    @pl.when(t == 0)
    def _():
        scr[0:HIST, :] = buf_ref[...]

    @pl.when(t > 0)
    def _():
        scr[0:HIST, :] = scr[tt:tt + HIST, :]

    scr[HIST:HIST + tt, :] = a_ref[...] * _sigmoid(ga_ref[...])

    def row_block(r, carry):
        r0 = pl.multiple_of(r * rb, rb)
        for c in range(nch):
            cs = slice(c * LANES, (c + 1) * LANES)
            slab = scr[pl.ds(r0, rb + HIST), cs]
            for shift in range(SUBLANES):
                n = (rb + HIST - shift) // SUBLANES * SUBLANES
                sh_scr[shift, 0:n, :] = slab[shift:shift + n, :]
            acc = jnp.zeros((rb, LANES), F32)
            for j in range(CONV_W):
                shift = (j + LEAD) % SUBLANES
                off = j + LEAD - shift
                acc = acc + w_ref[j:j + 1, cs] * sh_scr[shift, off:off + rb, :]
            c_scr[pl.ds(r0, rb), cs] = acc + cb_ref[:, cs]
        return carry

    lax.fori_loop(0, tt // rb, row_block, 0)

    cv = c_scr[...]
    mu = jnp.mean(cv, axis=-1, keepdims=True)
    xc = cv - mu
    var = jnp.mean(xc * xc, axis=-1, keepdims=True)
    y = xc * lax.rsqrt(var + EPS) * g_ref[...] + b_ref[...]
    out_ref[...] = (_silu(y) * _silu(za_ref[...])).astype(out_ref.dtype)

    @pl.when(t == pl.num_programs(1) - 1)
    def _():
        nb_ref[...] = scr[tt:tt + HIST, :]


def _conv_branch(h3, buf, cw, cb, g, b, tt):
    bsz, tlen, _ = h3.shape
    c = cw.shape[-1]
    assert tlen % tt == 0
    rb = min(CONV_ROW_BLOCK, tt)
    assert tt % rb == 0
    row = lambda col: pl.BlockSpec((None, tt, c), lambda bi, ti: (bi, ti, col))
    vec = lambda r: pl.BlockSpec((r, c), lambda bi, ti: (0, 0))
    hist = pl.BlockSpec((None, HIST, c), lambda bi, ti: (bi, 0, 0))
    return pl.pallas_call(
        functools.partial(_conv_kernel, tt=tt, rb=rb),
        grid=(bsz, tlen // tt),
        in_specs=[row(0), row(1), row(2), hist, vec(CONV_W), vec(1), vec(1), vec(1)],
        out_specs=[pl.BlockSpec((None, tt, c), lambda bi, ti: (bi, ti, 0)), hist],
        out_shape=[jax.ShapeDtypeStruct((bsz, tlen, c), MIXER_OUT_DTYPE), jax.ShapeDtypeStruct((bsz, HIST, c), F32)],
        scratch_shapes=[pltpu.VMEM((tt + HIST, c), F32), pltpu.VMEM((tt, c), F32),
                        pltpu.VMEM((SUBLANES, rb + HIST, LANES), F32)],
        compiler_params=_params(("parallel", "arbitrary")),
        name="conv_branch",
    )(h3, h3, h3, buf, cw, cb, g, b)


def _cumsum_rows(tri3, g):
    g0 = g.astype(BF16)
    r1 = g - g0.astype(F32)
    g1 = r1.astype(BF16)
    g2 = (r1 - g1.astype(F32)).astype(BF16)
    return _dot(tri3, jnp.concatenate([g0, g1, g2], axis=0))


def _pair_reference(x, h):
    nrows, width = x.shape
    if h >= SUBLANES:
        parts = [jnp.broadcast_to(x[p + h - 1:p + h], (2 * h, width)) for p in range(0, nrows, 2 * h)]
        return parts[0] if len(parts) == 1 else jnp.concatenate(parts, axis=0)
    x3 = x.reshape(nrows // SUBLANES, SUBLANES, width)
    sub = lax.broadcasted_iota(jnp.int32, (1, SUBLANES, 1), 1)
    ref = None
    for p in range(SUBLANES - 2 * h, -1, -2 * h):
        row = x3[:, p + h - 1:p + h, :]
        ref = row if ref is None else jnp.where(sub < p + 2 * h, row, ref)
    return jnp.broadcast_to(ref, x3.shape).reshape(nrows, width)


def _hgrn_kernel(q_ref, f_ref, i_ref, zb_ref, lbl_ref, hg_ref, s0_ref, o_ref, s_ref, st_scr,
                 *, layer, chunk, nchunks, heads):
    t = pl.program_id(2)
    dk = HB_DK

    @pl.when(t == 0)
    def _():
        for h in range(heads):
            st_scr[h] = s0_ref[h].T

    lg = lbl_ref[...]
    ex = jnp.exp(lg - jnp.max(lg, axis=0, keepdims=True))
    sm = ex / jnp.sum(ex, axis=0, keepdims=True)
    lb = jnp.zeros_like(sm[0:1])
    for r in range(1, layer + 1):
        lb = lb + sm[r:r + 1]
    log_lb = jnp.log(lb)
    log1m_lb = jnp.log1p(-lb)

    rows = lax.broadcasted_iota(jnp.int32, (chunk, chunk), 0)
    cols = lax.broadcasted_iota(jnp.int32, (chunk, chunk), 1)
    tri = jnp.where(rows >= cols, 1.0, 0.0).astype(BF16)
    tri3 = jnp.concatenate([tri, tri, tri], axis=1)
    rows2 = lax.broadcasted_iota(jnp.int32, (chunk, 2 * chunk), 0)
    cols2 = lax.rem(lax.broadcasted_iota(jnp.int32, (chunk, 2 * chunk), 1), chunk)
    xor2 = jnp.bitwise_xor(rows2, cols2)
    levels = [0]
    h = 1
    while h < chunk:
        levels.append(h)
        h *= 2
    heads_cols = [slice(hd * dk, (hd + 1) * dk) for hd in range(heads)]
    zero_blk = jnp.zeros((chunk, dk), BF16)

    def block_diag(x, g):
        a, b = x[:, heads_cols[2 * g]], x[:, heads_cols[2 * g + 1]]
        return jnp.concatenate([jnp.concatenate([a, zero_blk], axis=1), jnp.concatenate([zero_blk, b], axis=1)],
                               axis=0)

    def one_chunk(ci, carry):
        r0 = pl.multiple_of(ci * chunk, chunk)
        rs = pl.ds(r0, chunk)
        fr = f_ref[rs, :]
        lsig = jnp.minimum(fr, 0.0) - jnp.log(1.0 + jnp.exp(-jnp.abs(fr)))
        c2 = log1m_lb + lsig
        logf = jnp.maximum(log_lb, c2) + jnp.log(1.0 + jnp.exp(-jnp.abs(log_lb - c2)))
        kk = (1.0 - lb) * _sigmoid(-fr)
        qq = _silu(q_ref[rs, :])
        vv = i_ref[rs, :]
        vb = vv.astype(BF16)
        bl = _cumsum_rows(tri3, logf) * LOG2E
        qe = (qq * jnp.exp2(bl)).astype(BF16)
        o = jnp.concatenate([_dot_nt(qe[:, cs], st_scr[hd].astype(BF16)) for hd, cs in enumerate(heads_cols)],
                            axis=1)
        att = [jnp.zeros((chunk, 2 * chunk), F32) for _ in range(heads // 2)]
        for h in levels:
            if h == 0:
                qf, kf, keep = qq.astype(BF16), kk.astype(BF16), rows2 == cols2
            else:
                fac = jnp.exp2(-jnp.abs(bl - _pair_reference(bl, h)))
                qf, kf = (qq * fac).astype(BF16), (kk * fac).astype(BF16)
                keep = (xor2 >= h) & (xor2 < 2 * h) & (rows2 > cols2)
            for g in range(heads // 2):
                a_h = _dot_nt(qf[:, 2 * g * dk:(2 * g + 2) * dk], block_diag(kf, g))
                att[g] = jnp.where(keep, a_h, att[g])
        o = o + jnp.concatenate([_dot(att[g].astype(BF16), block_diag(vb, g)) for g in range(heads // 2)], axis=1)
        bl_last = bl[chunk - 1:chunk]
        kd = (kk * jnp.exp2(bl_last - bl)).astype(BF16)
        decay = jnp.exp2(bl_last)
        for hd, cs in enumerate(heads_cols):
            st_scr[hd] = st_scr[hd] * decay[:, cs] + _dot_tn(vb[:, cs], kd[:, cs])
        inv = jnp.concatenate(
            [jnp.broadcast_to(lax.rsqrt(jnp.mean(o[:, cs] * o[:, cs], axis=-1, keepdims=True) + EPS), (chunk, dk))
             for cs in heads_cols], axis=1)
        o_ref[rs, :] = (o * inv * jnp.concatenate([hg_ref[...]] * heads, axis=1)
                        * _silu(zb_ref[rs, :])).astype(o_ref.dtype)
        return carry

    lax.fori_loop(0, nchunks, one_chunk, 0)

    @pl.when(t == pl.num_programs(2) - 1)
    def _():
        for h in range(heads):
            s_ref[h] = st_scr[h].T


def _hgrn_branch(h3, lb_logits, hn_g, s0, layer, col_q, tc, heads):
    bsz, tlen, _ = h3.shape
    nh, dk, dv = s0.shape[1:]
    d_b = nh * dk
    chunk = CHUNK if tlen % CHUNK == 0 else tlen
    assert chunk % SUBLANES == 0 and chunk & (chunk - 1) == 0
    assert tc % chunk == 0 and tlen % tc == 0 and nh % heads == 0
    w = heads * dk
    gpb = d_b // w
    col = lambda k: pl.BlockSpec((None, tc, w), lambda bi, hi, ti: (bi, ti, (col_q + k) * gpb + hi))
    st = pl.BlockSpec((None, heads, dk, dv), lambda bi, hi, ti: (bi, hi, 0, 0))
    ne = lb_logits.shape[0]
    return pl.pallas_call(
        functools.partial(_hgrn_kernel, layer=layer, chunk=chunk, nchunks=tc // chunk, heads=heads),
        grid=(bsz, nh // heads, tlen // tc),
        in_specs=[col(0), col(1), col(2), col(3),
                  pl.BlockSpec((ne, w), lambda bi, hi, ti: (0, hi)),
                  pl.BlockSpec((1, dv), lambda bi, hi, ti: (0, 0)),
                  st],
        out_specs=[pl.BlockSpec((None, tc, w), lambda bi, hi, ti: (bi, ti, hi)), st],
        out_shape=[jax.ShapeDtypeStruct((bsz, tlen, d_b), MIXER_OUT_DTYPE), jax.ShapeDtypeStruct(s0.shape, F32)],
        scratch_shapes=[pltpu.VMEM((heads, dv, dk), F32)],
        compiler_params=_params(("parallel", "parallel", "arbitrary")),
        name="hgrn_branch",
    )(h3, h3, h3, h3, lb_logits, hn_g, s0)


def _out_ln_kernel(a_ref, b_ref, w_ref, x_ref, g_ref, bb_ref, o_ref, *, alpha):
    ka = a_ref.shape[-1]
    tm = x_ref.shape[0]
    halves = [pl.ds(i * (tm // 2), tm // 2) for i in range(2)] if tm % 16 == 0 else [pl.ds(0, tm)]
    ys = [_dot(a_ref[rs, :].astype(BF16), w_ref[0:ka, :]) + _dot(b_ref[rs, :].astype(BF16), w_ref[ka:, :])
          for rs in halves]
    for rs, y in zip(halves, ys):
        r = alpha * x_ref[rs, :] + y
        mu = jnp.mean(r, axis=-1, keepdims=True)
        rc = r - mu
        var = jnp.mean(rc * rc, axis=-1, keepdims=True)
        o_ref[rs, :] = rc * lax.rsqrt(var + EPS) * g_ref[...] + bb_ref[...]


def _out_ln(a, a_col, b, b_col, w, x, g, bb, alpha, tm):
    n, d = x.shape
    kh = w.shape[0] // 2
    assert n % tm == 0
    return pl.pallas_call(
        functools.partial(_out_ln_kernel, alpha=alpha),
        grid=(n // tm,),
        in_specs=[pl.BlockSpec((tm, kh), lambda i: (i, a_col)),
                  pl.BlockSpec((tm, kh), lambda i: (i, b_col)),
                  pl.BlockSpec(w.shape, lambda i: (0, 0)),
                  pl.BlockSpec((tm, d), lambda i: (i, 0)),
                  pl.BlockSpec((1, d), lambda i: (0, 0)),
                  pl.BlockSpec((1, d), lambda i: (0, 0))],
        out_specs=pl.BlockSpec((tm, d), lambda i: (i, 0)),
        out_shape=jax.ShapeDtypeStruct((n, d), F32),
        compiler_params=_params(("parallel",)),
        name="out_ln",
    )(a, b, w, x, g, bb)


def _lambda(lq1, lk1, lq2, lk2, lam_init):
    e1 = jnp.exp(jnp.sum(lq1[...] * lk1[...], axis=-1, keepdims=True))
    e2 = jnp.exp(jnp.sum(lq2[...] * lk2[...], axis=-1, keepdims=True))
    return e1 - e2 + lam_init


def _map_queries(q, scale):
    q = q * scale
    lane = lax.broadcasted_iota(jnp.int32, q.shape, 1)
    return jnp.concatenate([jnp.where(lane < HC_DIM, q, 0.0), jnp.where(lane >= HC_DIM, q, 0.0)],
                           axis=0).astype(BF16)


def _attn_finish(o1, o2, lam, sg_ref, z, o_ref, cols, lam_init):
    o = o1 - lam * o2
    o = o * lax.rsqrt(jnp.mean(o * o, axis=-1, keepdims=True) + EPS) * sg_ref[...] * (1.0 - lam_init)
    o_ref[:, cols] = (o * _silu(z)).astype(o_ref.dtype)


def _attn_prompt_kernel(q_ref, k_ref, v_ref, z_ref, lq1, lk1, lq2, lk2, sg_ref, o_ref,
                        kb_scr, vb_scr, m_scr, acc_scr, s_scr, bias_scr, *, tq, tk, lam_init):
    qi = pl.program_id(2)
    tlen = k_ref.shape[0]

    @pl.when(qi == 0)
    def _():
        kb_scr[...] = k_ref[...].astype(BF16)
        vb_scr[:, 0:LANES] = v_ref[...].astype(BF16)
        vb_scr[:, LANES:] = jnp.ones((tlen, LANES), BF16)
        q_chunk = lax.div(lax.broadcasted_iota(jnp.int32, (tq, 1), 0), CHUNK)
        k_chunk = lax.div(lax.broadcasted_iota(jnp.int32, (1, tk), 1), CHUNK)
        bias_scr[...] = jnp.where(k_chunk <= q_chunk, 0.0, -jnp.inf)

    q2 = _map_queries(q_ref[...], (HC_DIM ** -0.5) * LOG2E)
    m_scr[...] = jnp.full(m_scr.shape, -jnp.inf, F32)
    acc_scr[...] = jnp.zeros(acc_scr.shape, F32)

    def scores(kj):
        k0 = pl.multiple_of(kj * tk, tk)
        return _dot_nt(q2, kb_scr[pl.ds(k0, tk), :])

    def consume(s, rows, k_start, nkeys):
        tiles = [s[:, t * LANES:(t + 1) * LANES] for t in range(nkeys // LANES)]
        m_cur = tiles[0]
        for tl in tiles[1:]:
            m_cur = jnp.maximum(m_cur, tl)
        m_old = m_scr[rows, :]
        m_new = jnp.maximum(m_old, jnp.max(m_cur, axis=-1, keepdims=True))
        alpha = jnp.exp2(m_old - m_new)
        p = jnp.concatenate([jnp.exp2(tl - m_new).astype(BF16) for tl in tiles], axis=1)
        acc_scr[rows, :] = (jnp.concatenate([alpha, alpha], axis=1) * acc_scr[rows, :]
                            + _dot(p, vb_scr[pl.ds(k_start, nkeys), :]))
        m_scr[rows, :] = m_new

    def block(kj, prefetch):
        s = s_scr[...]
        if prefetch is not None:
            s_scr[...] = scores(prefetch)
        consume(s, slice(None), pl.multiple_of(kj * tk, tk), tk)

    def diagonal_block():
        half = tq // 2
        k_lo = pl.multiple_of(qi * tk, tk)
        k_hi = pl.multiple_of(qi * tk + half, half)
        bias = bias_scr[...]
        s_lo = _dot_nt(q2, kb_scr[pl.ds(k_lo, half), :]) + jnp.concatenate([bias[:, 0:half]] * 2, axis=0)
        q_hi = jnp.concatenate([q2[half:tq], q2[tq + half:]], axis=0)
        s_hi = _dot_nt(q_hi, kb_scr[pl.ds(k_hi, half), :]) + jnp.concatenate([bias[half:, half:]] * 2, axis=0)
        s_scr[...] = scores(0)
        consume(s_lo, slice(None), k_lo, half)
        consume(s_hi[0:half], slice(half, tq), k_hi, half)
        consume(s_hi[half:], slice(tq + half, 2 * tq), k_hi, half)

    def two_blocks(i, carry):
        block(2 * i, 2 * i + 1)
        block(2 * i + 1, 2 * i + 2)
        return carry

    diagonal_block()
    npre = qi - 1
    lax.fori_loop(0, npre // 2, two_blocks, 0)

    @pl.when((npre > 0) & (npre % 2 == 1))
    def _():
        block(npre - 1, npre)

    @pl.when(qi > 0)
    def _():
        block(qi - 1, None)

    lam = _lambda(lq1, lk1, lq2, lk2, lam_init)
    acc = acc_scr[...]
    _attn_finish(acc[0:tq, 0:LANES] / acc[0:tq, LANES:], acc[tq:, 0:LANES] / acc[tq:, LANES:],
                 lam, sg_ref, z_ref[...], o_ref, slice(None), lam_init)


def _attn_prompt(q3, k3, v3, z3, lq1, lk1, lq2, lk2, sg, lam_init, tq, tk):
    bsz, tlen, hd = q3.shape
    nh = hd // LANES
    assert tlen % tk == 0 and tk == tq and tq % CHUNK == 0
    qblk = pl.BlockSpec((None, tq, LANES), lambda bi, hi, qi: (bi, qi, hi))
    kblk = pl.BlockSpec((None, tlen, LANES), lambda bi, hi, qi: (bi, 0, hi))
    vec = lambda n: pl.BlockSpec((1, n), lambda bi, hi, qi: (0, 0))
    return pl.pallas_call(
        functools.partial(_attn_prompt_kernel, tq=tq, tk=tk, lam_init=lam_init),
        grid=(bsz, nh, tlen // tq),
        in_specs=[qblk, kblk, kblk, qblk, vec(HC_DIM), vec(HC_DIM), vec(HC_DIM), vec(HC_DIM), vec(LANES)],
        out_specs=qblk,
        out_shape=jax.ShapeDtypeStruct((bsz, tlen, hd), MIXER_OUT_DTYPE),
        scratch_shapes=[pltpu.VMEM((tlen, LANES), BF16), pltpu.VMEM((tlen, 2 * LANES), BF16),
                        pltpu.VMEM((2 * tq, LANES), F32), pltpu.VMEM((2 * tq, 2 * LANES), F32),
                        pltpu.VMEM((2 * tq, tk), F32), pltpu.VMEM((tq, tk), F32)],
        compiler_params=_params(("parallel", "parallel", "arbitrary")),
        name="attn_prompt",
    )(q3, k3, v3, z3, lq1, lk1, lq2, lk2, sg)


def _attn_sample_kernel(q_ref, ck_ref, cv_ref, kn_ref, vn_ref, z_ref, lq1, lk1, lq2, lk2, sg_ref, o_ref,
                        m_scr, l_scr, acc_scr, *, lam_init, past):
    kb = pl.program_id(2)
    tlen = q_ref.shape[0]
    tkc, heads = ck_ref.shape[0], ck_ref.shape[1]
    per_head = 2 * tlen
    ncol = heads * per_head

    @pl.when(kb == 0)
    def _():
        m_scr[...] = jnp.full(m_scr.shape, -jnp.inf, F32)
        l_scr[...] = jnp.zeros(l_scr.shape, F32)
        acc_scr[...] = jnp.zeros(acc_scr.shape, F32)

    head_cols = [slice(h * LANES, (h + 1) * LANES) for h in range(heads)]
    scale = (HC_DIM ** -0.5) * LOG2E
    qw = jnp.concatenate([_map_queries(q_ref[:, cs], scale) for cs in head_cols], axis=0)
    col = lax.broadcasted_iota(jnp.int32, (1, ncol), 1)
    col_head = lax.div(col, per_head)
    q_chunk = lax.div(past + lax.rem(col, tlen), CHUNK)

    def update(k_ref, v_ref, row_head, row_pos):
        k, v = k_ref().astype(BF16), v_ref().astype(BF16)
        n, cyc = k.shape[0], row_head.shape[0]
        own = jnp.where(row_head == col_head, 0.0, -jnp.inf)
        pieces = DECODE_ROW_PIECES if n % (DECODE_ROW_PIECES * cyc) == 0 else 1
        step = n // pieces
        scores = lambda i: _dot_nt(k[i * step:(i + 1) * step], qw)
        s_next = scores(0)
        for i in range(pieces):
            s = s_next
            if i + 1 < pieces:
                s_next = scores(i + 1)
            s = (s.reshape(step // cyc, cyc, ncol) + own[None]).reshape(step, ncol)
            if row_pos is not None:
                s = jnp.where(lax.div(row_pos[i * step:(i + 1) * step], CHUNK) <= q_chunk, s, -jnp.inf)
            m_old = m_scr[...]
            m_new = jnp.maximum(m_old, jnp.max(s, axis=0, keepdims=True))
            alpha = jnp.exp2(m_old - m_new)
            p = jnp.exp2(s - m_new)
            l_scr[...] = alpha * l_scr[...] + jnp.sum(p, axis=0, keepdims=True)
            acc_scr[...] = alpha * acc_scr[...] + _dot_tn(v[i * step:(i + 1) * step], p.astype(BF16))
            m_scr[...] = m_new

    cached_k = lambda: ck_ref[...].reshape(tkc * heads, LANES)
    cached_v = lambda: cv_ref[...].reshape(tkc * heads, LANES)
    cached_head = lax.broadcasted_iota(jnp.int32, (heads, 1), 0)
    all_visible = (kb * tkc + tkc - 1) // CHUNK <= past // CHUNK

    @pl.when(all_visible)
    def _():
        update(cached_k, cached_v, cached_head, None)

    @pl.when(jnp.logical_not(all_visible))
    def _():
        row = lax.broadcasted_iota(jnp.int32, (tkc * heads, 1), 0)
        update(cached_k, cached_v, cached_head, kb * tkc + lax.div(row, heads))

    @pl.when(kb == pl.num_programs(2) - 1)
    def _():
        row_n = lax.broadcasted_iota(jnp.int32, (tlen * heads, 1), 0)
        update(lambda: jnp.concatenate([kn_ref[:, cs] for cs in head_cols], axis=0),
               lambda: jnp.concatenate([vn_ref[:, cs] for cs in head_cols], axis=0),
               lax.div(row_n, tlen), past + lax.rem(row_n, tlen))
        o = (acc_scr[...] / l_scr[...]).T
        lam = _lambda(lq1, lk1, lq2, lk2, lam_init)
        for h, cs in enumerate(head_cols):
            o_h = o[h * per_head:(h + 1) * per_head]
            _attn_finish(o_h[0:tlen], o_h[tlen:], lam, sg_ref, z_ref[:, cs], o_ref, cs, lam_init)


def _attn_sample(q3, ck5, cv5, layer, kn3, vn3, z3, lq1, lk1, lq2, lk2, sg, lam_init, tkc, heads):
    bsz, tlen, hd = q3.shape
    past, nh = ck5.shape[2], ck5.shape[3]
    assert past % tkc == 0 and nh % heads == 0
    ncol = heads * 2 * tlen
    new = pl.BlockSpec((None, tlen, heads * LANES), lambda bi, hi, ki: (bi, 0, hi))
    old = pl.BlockSpec((None, None, tkc, heads, LANES), lambda bi, hi, ki: (layer, bi, ki, hi, 0))
    vec = lambda n: pl.BlockSpec((1, n), lambda bi, hi, ki: (0, 0))
    stat = pltpu.VMEM((1, ncol), F32)
    return pl.pallas_call(
        functools.partial(_attn_sample_kernel, lam_init=lam_init, past=past),
        grid=(bsz, nh // heads, past // tkc),
        in_specs=[new, old, old, new, new, new, vec(HC_DIM), vec(HC_DIM), vec(HC_DIM), vec(HC_DIM), vec(LANES)],
        out_specs=new,
        out_shape=jax.ShapeDtypeStruct((bsz, tlen, hd), MIXER_OUT_DTYPE),
        scratch_shapes=[stat, stat, pltpu.VMEM((LANES, ncol), F32)],
        compiler_params=_params(("parallel", "parallel", "arbitrary")),
        name="attn_sample",
    )(q3, ck5, cv5, kn3, vn3, z3, lq1, lk1, lq2, lk2, sg)


def _row_tile(n, pref):
    while n % pref:
        pref //= 2
    return pref


def _even_layer(x3, conv_buf, s0, w_in, w_out, cw, cb, cg, cbb, lb_logits, hn_g, pg, pb, layer, alpha):
    bsz, tlen, d = x3.shape
    n = bsz * tlen
    x2 = x3.reshape(n, d)
    d_a = cw.shape[-1]
    (h,) = _linear(x2, w_in, 1, _row_tile(n, LINEAR_ROWS), w_in.shape[1] // EVEN_PROJ_COL_TILES)
    h3 = h.reshape(bsz, tlen, -1)
    buf = jnp.pad(conv_buf, ((0, 0), (LEAD, 0), (0, 0)))
    out_a, nbuf = _conv_branch(h3, buf, cw, cb, cg, cbb, _row_tile(tlen, CONV_TIME))
    tc = _row_tile(tlen, HGRN_TIME) if tlen % CHUNK == 0 else tlen
    d_b = s0.shape[1] * s0.shape[2]
    out_b, s_new = _hgrn_branch(h3, lb_logits, hn_g, s0, layer, 3 * d_a // d_b, tc, HGRN_HEADS_PER_STEP)
    xn = _out_ln(out_a.reshape(n, -1), 0, out_b.reshape(n, -1), 0, w_out, x2, pg, pb, alpha,
                 _row_tile(n, OUT_LN_ROWS))
    return xn.reshape(bsz, tlen, d), nbuf[:, LEAD:], s_new


def _odd_layer(x3, caches, w_in, w_out, lq1, lk1, lq2, lk2, sg, pg, pb, lam_init, alpha):
    bsz, tlen, d = x3.shape
    n = bsz * tlen
    x2 = x3.reshape(n, d)
    d_c = w_in.shape[1] // 4
    q, k, v, z = [y.reshape(bsz, tlen, d_c)
                  for y in _linear(x2, w_in, 4, _row_tile(n, LINEAR_ROWS), _row_tile(d_c, LINEAR_COLS))]
    if caches is None:
        tk = _row_tile(tlen, ATTN_BLOCK)
        o = _attn_prompt(q, k, v, z, lq1, lk1, lq2, lk2, sg, lam_init, tk, tk)
    else:
        ck, cv, layer = caches
        o = _attn_sample(q, ck, cv, layer, k, v, z, lq1, lk1, lq2, lk2, sg, lam_init,
                         _row_tile(ck.shape[2], DECODE_KEYS), SUBLANES)
    o2 = o.reshape(n, d_c)
    xn = _out_ln(o2, 0, o2, 1, w_out, x2, pg, pb, alpha, _row_tile(n, OUT_LN_ROWS))
    nh = d_c // LANES
    return xn.reshape(bsz, tlen, d), k.reshape(bsz, tlen, nh, LANES), v.reshape(bsz, tlen, nh, LANES)


def kernel(x_prompt, x_sample, state_conv, state_hgrn, cache_k, cache_v, w_in_even, w_out_even, conv_w, conv_b,
           conv_ln_g, conv_ln_b, hgrn_lb_logits, hgrn_norm_g, w_in_odd, w_out_odd, lam_q1, lam_k1, lam_q2, lam_k2,
           subln_g, post_ln_g, post_ln_b):
    depth = post_ln_g.shape[0]
    alpha = (2.0 * depth) ** 0.25
    xp, xs = x_prompt, x_sample
    bp = xp.shape[0]
    row = lambda a: a.reshape(1, -1)
    outs = {name: [] for name in ("conv_p", "hgrn_p", "k_p", "v_p", "conv_s", "hgrn_s", "k_s", "v_s")}
    for l in range(depth):
        pg, pb = row(post_ln_g[l]), row(post_ln_b[l])
        if l % 2 == 0:
            e = l // 2
            prm = (w_in_even[e].astype(BF16), w_out_even[e].astype(BF16), conv_w[e], row(conv_b[e]),
                   row(conv_ln_g[e]), row(conv_ln_b[e]), hgrn_lb_logits, row(hgrn_norm_g[e]), pg, pb, e, alpha)
            buf0 = jnp.zeros((bp,) + state_conv.shape[2:], F32)
            s0 = jnp.zeros((bp,) + state_hgrn.shape[2:], F32)
            xp, cb_p, s_p = _even_layer(xp, buf0, s0, *prm)
            xs, cb_s, s_s = _even_layer(xs, state_conv[e], state_hgrn[e], *prm)
            outs["conv_p"].append(cb_p); outs["hgrn_p"].append(s_p)
            outs["conv_s"].append(cb_s); outs["hgrn_s"].append(s_s)
        else:
            o = l // 2
            lam_init = 0.8 - 0.6 * math.exp(-0.3 * l)
            prm = (w_in_odd[o].astype(BF16), w_out_odd[o].astype(BF16), row(lam_q1[o]), row(lam_k1[o]),
                   row(lam_q2[o]), row(lam_k2[o]), row(subln_g[o]), pg, pb, lam_init, alpha)
            xp, kn_p, vn_p = _odd_layer(xp, None, *prm)
            xs, kn_s, vn_s = _odd_layer(xs, (cache_k, cache_v, o), *prm)
            outs["k_p"].append(kn_p); outs["v_p"].append(vn_p)
            outs["k_s"].append(kn_s); outs["v_s"].append(vn_s)
    st = lambda name: jnp.stack(outs[name])
    return (xp, xs, st("conv_p"), st("hgrn_p"), st("k_p"), st("v_p"),
            st("conv_s"), st("hgrn_s"), st("k_s"), st("v_s"))
```

```python
import functools
import math

import jax
import jax.numpy as jnp
from jax import lax
from jax.experimental import pallas as pl
from jax.experimental.pallas import tpu as pltpu

F32 = jnp.float32
BF16 = jnp.bfloat16
MIXER_OUT_DTYPE = BF16

CHUNK = 64
SUBLANES = 8
LANES = 128
CONV_W = 31
HIST = 32
LEAD = HIST - (CONV_W - 1)
HB_DK = 128
HC_DIM = 64
EPS = 1e-5
LOG2E = math.log2(math.e)
VMEM_LIMIT = 48 * 1024 * 1024

LINEAR_ROWS = 1024
LINEAR_COLS = 1024
EVEN_PROJ_COL_TILES = 4
OUT_LN_ROWS = 512
CONV_TIME = 256
CONV_ROW_BLOCK = 64
HGRN_TIME = 512
HGRN_HEADS_PER_STEP = 8
ATTN_BLOCK = 512
ATTN_RUN = 4
DECODE_KEYS = 1024
DECODE_ROW_PIECES = 4


def _params(sem):
    return pltpu.CompilerParams(dimension_semantics=sem, vmem_limit_bytes=VMEM_LIMIT)


def _dot(a, b):
    return jnp.dot(a, b, preferred_element_type=F32)


def _dot_nt(a, b):
    return lax.dot_general(a, b, (((1,), (1,)), ((), ())), preferred_element_type=F32)


def _dot_tn(a, b):
    return lax.dot_general(a, b, (((0,), (0,)), ((), ())), preferred_element_type=F32)


def _sigmoid(x):
    return 1.0 / (1.0 + jnp.exp(-x))


def _silu(x):
    return x * _sigmoid(x)


def _linear_kernel(x_ref, w_ref, *o_refs, blocks_per_out):
    j = pl.program_id(1)
    if len(o_refs) == 1:
        o_refs[0][...] = _dot(x_ref[...].astype(BF16), w_ref[...])
        return
    for k, o_ref in enumerate(o_refs):
        @pl.when((j >= k * blocks_per_out) & (j < (k + 1) * blocks_per_out))
        def _(o_ref=o_ref):
            o_ref[...] = _dot(x_ref[...].astype(BF16), w_ref[...])


def _linear(x, w, nout, tm, tn):
    n, k = x.shape
    width = w.shape[1] // nout
    assert n % tm == 0 and width % tn == 0 and width * nout == w.shape[1]
    bpo = width // tn

    def out_map(g):
        return lambda i, j: (i, jnp.clip(j - g * bpo, 0, bpo - 1))

    return pl.pallas_call(
        functools.partial(_linear_kernel, blocks_per_out=bpo),
        grid=(n // tm, nout * bpo),
        in_specs=[pl.BlockSpec((tm, k), lambda i, j: (i, 0)),
                  pl.BlockSpec((k, tn), lambda i, j: (0, j))],
        out_specs=[pl.BlockSpec((tm, tn), out_map(g)) for g in range(nout)],
        out_shape=[jax.ShapeDtypeStruct((n, width), F32) for _ in range(nout)],
        compiler_params=_params(("parallel", "arbitrary")),
        name="linear",
    )(x, w)


def _conv_kernel(a_ref, ga_ref, za_ref, buf_ref, w_ref, cb_ref, g_ref, b_ref, out_ref, nb_ref, scr, c_scr, sh_scr,
                 *, tt, rb):
    t = pl.program_id(1)
    nch = a_ref.shape[-1] // LANES

    @pl.when(t == 0)
    def _():
        scr[0:HIST, :] = buf_ref[...]

    @pl.when(t > 0)
    def _():
        scr[0:HIST, :] = scr[tt:tt + HIST, :]

    scr[HIST:HIST + tt, :] = a_ref[...] * _sigmoid(ga_ref[...])

    def row_block(r, carry):
        r0 = pl.multiple_of(r * rb, rb)
        for c in range(nch):
            cs = slice(c * LANES, (c + 1) * LANES)
            slab = scr[pl.ds(r0, rb + HIST), cs]
            for shift in range(SUBLANES):
                n = (rb + HIST - shift) // SUBLANES * SUBLANES
                sh_scr[shift, 0:n, :] = slab[shift:shift + n, :]
            acc = jnp.zeros((rb, LANES), F32)
            for j in range(CONV_W):
                shift = (j + LEAD) % SUBLANES
                off = j + LEAD - shift
                acc = acc + w_ref[j:j + 1, cs] * sh_scr[shift, off:off + rb, :]
            c_scr[pl.ds(r0, rb), cs] = acc + cb_ref[:, cs]
        return carry

    lax.fori_loop(0, tt // rb, row_block, 0)

    cv = c_scr[...]
    mu = jnp.mean(cv, axis=-1, keepdims=True)
    xc = cv - mu
    var = jnp.mean(xc * xc, axis=-1, keepdims=True)
    y = xc * lax.rsqrt(var + EPS) * g_ref[...] + b_ref[...]
    out_ref[...] = (_silu(y) * _silu(za_ref[...])).astype(out_ref.dtype)

    @pl.when(t == pl.num_programs(1) - 1)
    def _():
        nb_ref[...] = scr[tt:tt + HIST, :]


def _conv_branch(h3, buf, cw, cb, g, b, tt):
    bsz, tlen, _ = h3.shape
    c = cw.shape[-1]
    assert tlen % tt == 0
    rb = min(CONV_ROW_BLOCK, tt)
    assert tt % rb == 0
    row = lambda col: pl.BlockSpec((None, tt, c), lambda bi, ti: (bi, ti, col))
    vec = lambda r: pl.BlockSpec((r, c), lambda bi, ti: (0, 0))
    hist = pl.BlockSpec((None, HIST, c), lambda bi, ti: (bi, 0, 0))
    return pl.pallas_call(
        functools.partial(_conv_kernel, tt=tt, rb=rb),
        grid=(bsz, tlen // tt),
        in_specs=[row(0), row(1), row(2), hist, vec(CONV_W), vec(1), vec(1), vec(1)],
        out_specs=[pl.BlockSpec((None, tt, c), lambda bi, ti: (bi, ti, 0)), hist],
        out_shape=[jax.ShapeDtypeStruct((bsz, tlen, c), MIXER_OUT_DTYPE), jax.ShapeDtypeStruct((bsz, HIST, c), F32)],
        scratch_shapes=[pltpu.VMEM((tt + HIST, c), F32), pltpu.VMEM((tt, c), F32),
                        pltpu.VMEM((SUBLANES, rb + HIST, LANES), F32)],
        compiler_params=_params(("parallel", "arbitrary")),
        name="conv_branch",
    )(h3, h3, h3, buf, cw, cb, g, b)


def _cumsum_rows(tri3, g):
    g0 = g.astype(BF16)
    r1 = g - g0.astype(F32)
    g1 = r1.astype(BF16)
    g2 = (r1 - g1.astype(F32)).astype(BF16)
    return _dot(tri3, jnp.concatenate([g0, g1, g2], axis=0))


def _pair_reference(x, h):
    nrows, width = x.shape
    if h >= SUBLANES:
        parts = [jnp.broadcast_to(x[p + h - 1:p + h], (2 * h, width)) for p in range(0, nrows, 2 * h)]
        return parts[0] if len(parts) == 1 else jnp.concatenate(parts, axis=0)
    x3 = x.reshape(nrows // SUBLANES, SUBLANES, width)
    sub = lax.broadcasted_iota(jnp.int32, (1, SUBLANES, 1), 1)
    ref = None
    for p in range(SUBLANES - 2 * h, -1, -2 * h):
        row = x3[:, p + h - 1:p + h, :]
        ref = row if ref is None else jnp.where(sub < p + 2 * h, row, ref)
    return jnp.broadcast_to(ref, x3.shape).reshape(nrows, width)


def _hgrn_kernel(q_ref, f_ref, i_ref, zb_ref, lbl_ref, hg_ref, s0_ref, o_ref, s_ref, st_scr,
                 *, layer, chunk, nchunks, heads):
    t = pl.program_id(2)
    dk = HB_DK

    @pl.when(t == 0)
    def _():
        for h in range(heads):
            st_scr[h] = s0_ref[h].T

    lg = lbl_ref[...]
    ex = jnp.exp(lg - jnp.max(lg, axis=0, keepdims=True))
    sm = ex / jnp.sum(ex, axis=0, keepdims=True)
    lb = jnp.zeros_like(sm[0:1])
    for r in range(1, layer + 1):
        lb = lb + sm[r:r + 1]
    log_lb = jnp.log(lb)
    log1m_lb = jnp.log1p(-lb)

    rows = lax.broadcasted_iota(jnp.int32, (chunk, chunk), 0)
    cols = lax.broadcasted_iota(jnp.int32, (chunk, chunk), 1)
    tri = jnp.where(rows >= cols, 1.0, 0.0).astype(BF16)
    tri3 = jnp.concatenate([tri, tri, tri], axis=1)
    rows2 = lax.broadcasted_iota(jnp.int32, (chunk, 2 * chunk), 0)
    cols2 = lax.rem(lax.broadcasted_iota(jnp.int32, (chunk, 2 * chunk), 1), chunk)
    xor2 = jnp.bitwise_xor(rows2, cols2)
    levels = [0]
    h = 1
    while h < chunk:
        levels.append(h)
        h *= 2
    heads_cols = [slice(hd * dk, (hd + 1) * dk) for hd in range(heads)]
    zero_blk = jnp.zeros((chunk, dk), BF16)

    def block_diag(x, g):
        a, b = x[:, heads_cols[2 * g]], x[:, heads_cols[2 * g + 1]]
        return jnp.concatenate([jnp.concatenate([a, zero_blk], axis=1), jnp.concatenate([zero_blk, b], axis=1)],
                               axis=0)

    def one_chunk(ci, carry):
        r0 = pl.multiple_of(ci * chunk, chunk)
        rs = pl.ds(r0, chunk)
        fr = f_ref[rs, :]
        lsig = jnp.minimum(fr, 0.0) - jnp.log(1.0 + jnp.exp(-jnp.abs(fr)))
        c2 = log1m_lb + lsig
        logf = jnp.maximum(log_lb, c2) + jnp.log(1.0 + jnp.exp(-jnp.abs(log_lb - c2)))
        kk = (1.0 - lb) * _sigmoid(-fr)
        qq = _silu(q_ref[rs, :])
        vv = i_ref[rs, :]
        vb = vv.astype(BF16)
        bl = _cumsum_rows(tri3, logf) * LOG2E
        qe = (qq * jnp.exp2(bl)).astype(BF16)
        o = jnp.concatenate([_dot_nt(qe[:, cs], st_scr[hd].astype(BF16)) for hd, cs in enumerate(heads_cols)],
                            axis=1)
        att = [jnp.zeros((chunk, 2 * chunk), F32) for _ in range(heads // 2)]
        for h in levels:
            if h == 0:
                qf, kf, keep = qq.astype(BF16), kk.astype(BF16), rows2 == cols2
            else:
                fac = jnp.exp2(-jnp.abs(bl - _pair_reference(bl, h)))
                qf, kf = (qq * fac).astype(BF16), (kk * fac).astype(BF16)
                keep = (xor2 >= h) & (xor2 < 2 * h) & (rows2 > cols2)
            for g in range(heads // 2):
                a_h = _dot_nt(qf[:, 2 * g * dk:(2 * g + 2) * dk], block_diag(kf, g))
                att[g] = jnp.where(keep, a_h, att[g])
        o = o + jnp.concatenate([_dot(att[g].astype(BF16), block_diag(vb, g)) for g in range(heads // 2)], axis=1)
        bl_last = bl[chunk - 1:chunk]
        kd = (kk * jnp.exp2(bl_last - bl)).astype(BF16)
        decay = jnp.exp2(bl_last)
        for hd, cs in enumerate(heads_cols):
            st_scr[hd] = st_scr[hd] * decay[:, cs] + _dot_tn(vb[:, cs], kd[:, cs])
        inv = jnp.concatenate(
            [jnp.broadcast_to(lax.rsqrt(jnp.mean(o[:, cs] * o[:, cs], axis=-1, keepdims=True) + EPS), (chunk, dk))
             for cs in heads_cols], axis=1)
        o_ref[rs, :] = (o * inv * jnp.concatenate([hg_ref[...]] * heads, axis=1)
                        * _silu(zb_ref[rs, :])).astype(o_ref.dtype)
        return carry

    lax.fori_loop(0, nchunks, one_chunk, 0)

    @pl.when(t == pl.num_programs(2) - 1)
    def _():
        for h in range(heads):
            s_ref[h] = st_scr[h].T


def _hgrn_branch(h3, lb_logits, hn_g, s0, layer, col_q, tc, heads):
    bsz, tlen, _ = h3.shape
    nh, dk, dv = s0.shape[1:]
    d_b = nh * dk
    chunk = CHUNK if tlen % CHUNK == 0 else tlen
    assert chunk % SUBLANES == 0 and chunk & (chunk - 1) == 0
    assert tc % chunk == 0 and tlen % tc == 0 and nh % heads == 0
    w = heads * dk
    gpb = d_b // w
    col = lambda k: pl.BlockSpec((None, tc, w), lambda bi, hi, ti: (bi, ti, (col_q + k) * gpb + hi))
    st = pl.BlockSpec((None, heads, dk, dv), lambda bi, hi, ti: (bi, hi, 0, 0))
    ne = lb_logits.shape[0]
    return pl.pallas_call(
        functools.partial(_hgrn_kernel, layer=layer, chunk=chunk, nchunks=tc // chunk, heads=heads),
        grid=(bsz, nh // heads, tlen // tc),
        in_specs=[col(0), col(1), col(2), col(3),
                  pl.BlockSpec((ne, w), lambda bi, hi, ti: (0, hi)),
                  pl.BlockSpec((1, dv), lambda bi, hi, ti: (0, 0)),
                  st],
        out_specs=[pl.BlockSpec((None, tc, w), lambda bi, hi, ti: (bi, ti, hi)), st],
        out_shape=[jax.ShapeDtypeStruct((bsz, tlen, d_b), MIXER_OUT_DTYPE), jax.ShapeDtypeStruct(s0.shape, F32)],
        scratch_shapes=[pltpu.VMEM((heads, dv, dk), F32)],
        compiler_params=_params(("parallel", "parallel", "arbitrary")),
        name="hgrn_branch",
    )(h3, h3, h3, h3, lb_logits, hn_g, s0)


def _out_ln_kernel(a_ref, b_ref, w_ref, x_ref, g_ref, bb_ref, o_ref, *, alpha):
    ka = a_ref.shape[-1]
    tm = x_ref.shape[0]
    halves = [pl.ds(i * (tm // 2), tm // 2) for i in range(2)] if tm % 16 == 0 else [pl.ds(0, tm)]
    ys = [_dot(a_ref[rs, :].astype(BF16), w_ref[0:ka, :]) + _dot(b_ref[rs, :].astype(BF16), w_ref[ka:, :])
          for rs in halves]
    for rs, y in zip(halves, ys):
        r = alpha * x_ref[rs, :] + y
        mu = jnp.mean(r, axis=-1, keepdims=True)
        rc = r - mu
        var = jnp.mean(rc * rc, axis=-1, keepdims=True)
        o_ref[rs, :] = rc * lax.rsqrt(var + EPS) * g_ref[...] + bb_ref[...]


def _out_ln(a, a_col, b, b_col, w, x, g, bb, alpha, tm):
    n, d = x.shape
    kh = w.shape[0] // 2
    assert n % tm == 0
    return pl.pallas_call(
        functools.partial(_out_ln_kernel, alpha=alpha),
        grid=(n // tm,),
        in_specs=[pl.BlockSpec((tm, kh), lambda i: (i, a_col)),
                  pl.BlockSpec((tm, kh), lambda i: (i, b_col)),
                  pl.BlockSpec(w.shape, lambda i: (0, 0)),
                  pl.BlockSpec((tm, d), lambda i: (i, 0)),
                  pl.BlockSpec((1, d), lambda i: (0, 0)),
                  pl.BlockSpec((1, d), lambda i: (0, 0))],
        out_specs=pl.BlockSpec((tm, d), lambda i: (i, 0)),
        out_shape=jax.ShapeDtypeStruct((n, d), F32),
        compiler_params=_params(("parallel",)),
        name="out_ln",
    )(a, b, w, x, g, bb)


def _lambda(lq1, lk1, lq2, lk2, lam_init):
    e1 = jnp.exp(jnp.sum(lq1[...] * lk1[...], axis=-1, keepdims=True))
    e2 = jnp.exp(jnp.sum(lq2[...] * lk2[...], axis=-1, keepdims=True))
    return e1 - e2 + lam_init


def _map_queries(q, scale):
    q = q * scale
    lane = lax.broadcasted_iota(jnp.int32, q.shape, 1)
    return jnp.concatenate([jnp.where(lane < HC_DIM, q, 0.0), jnp.where(lane >= HC_DIM, q, 0.0)],
                           axis=0).astype(BF16)


def _attn_finish(o1, o2, lam, sg_ref, z, o_ref, cols, lam_init):
    o = o1 - lam * o2
    o = o * lax.rsqrt(jnp.mean(o * o, axis=-1, keepdims=True) + EPS) * sg_ref[...] * (1.0 - lam_init)
    o_ref[:, cols] = (o * _silu(z)).astype(o_ref.dtype)


def _attn_prompt_kernel(q_ref, k_ref, v_ref, z_ref, lq1, lk1, lq2, lk2, sg_ref, o_ref,
                        kb_scr, vb_scr, m_scr, acc_scr, s_scr, bias_scr, *, tq, tk, lam_init):
    qi = pl.program_id(2)
    tlen = k_ref.shape[0]

    @pl.when(qi == 0)
    def _():
        kb_scr[...] = k_ref[...].astype(BF16)
        vb_scr[:, 0:LANES] = v_ref[...].astype(BF16)
        vb_scr[:, LANES:] = jnp.ones((tlen, LANES), BF16)
        q_chunk = lax.div(lax.broadcasted_iota(jnp.int32, (tq, 1), 0), CHUNK)
        k_chunk = lax.div(lax.broadcasted_iota(jnp.int32, (1, tk), 1), CHUNK)
        bias_scr[...] = jnp.where(k_chunk <= q_chunk, 0.0, -jnp.inf)

    q2 = _map_queries(q_ref[...], (HC_DIM ** -0.5) * LOG2E)
    m_scr[...] = jnp.full(m_scr.shape, -jnp.inf, F32)
    acc_scr[...] = jnp.zeros(acc_scr.shape, F32)

    def scores(kj):
        k0 = pl.multiple_of(kj * tk, tk)
        return _dot_nt(q2, kb_scr[pl.ds(k0, tk), :])

    def consume(s, rows, k_start, nkeys):
        tiles = [s[:, t * LANES:(t + 1) * LANES] for t in range(nkeys // LANES)]
        m_cur = tiles[0]
        for tl in tiles[1:]:
            m_cur = jnp.maximum(m_cur, tl)
        m_old = m_scr[rows, :]
        m_new = jnp.maximum(m_old, jnp.max(m_cur, axis=-1, keepdims=True))
        alpha = jnp.exp2(m_old - m_new)
        p = jnp.concatenate([jnp.exp2(tl - m_new).astype(BF16) for tl in tiles], axis=1)
        acc_scr[rows, :] = (jnp.concatenate([alpha, alpha], axis=1) * acc_scr[rows, :]
                            + _dot(p, vb_scr[pl.ds(k_start, nkeys), :]))
        m_scr[rows, :] = m_new

    def block(kj, prefetch):
        s = s_scr[...]
        if prefetch is not None:
            s_scr[...] = scores(prefetch)
        consume(s, slice(None), pl.multiple_of(kj * tk, tk), tk)

    def diagonal_block():
        half = tq // 2
        k_lo = pl.multiple_of(qi * tk, tk)
        k_hi = pl.multiple_of(qi * tk + half, half)
        bias = bias_scr[...]
        s_lo = _dot_nt(q2, kb_scr[pl.ds(k_lo, half), :]) + jnp.concatenate([bias[:, 0:half]] * 2, axis=0)
        q_hi = jnp.concatenate([q2[half:tq], q2[tq + half:]], axis=0)
        s_hi = _dot_nt(q_hi, kb_scr[pl.ds(k_hi, half), :]) + jnp.concatenate([bias[half:, half:]] * 2, axis=0)
        s_scr[...] = scores(0)
        consume(s_lo, slice(None), k_lo, half)
        consume(s_hi[0:half], slice(half, tq), k_hi, half)
        consume(s_hi[half:], slice(tq + half, 2 * tq), k_hi, half)

    def run(first, count, last_prefetches):
        for u in range(count):
            more = u + 1 < count or last_prefetches
            block(first + u, first + u + 1 if more else None)

    def full_run(i, carry):
        run(ATTN_RUN * i, ATTN_RUN, True)
        return carry

    diagonal_block()
    tail = lax.rem(qi - 1, ATTN_RUN) + 1
    lax.fori_loop(0, (qi - tail) // ATTN_RUN, full_run, 0)
    for t in range(1, ATTN_RUN + 1):
        @pl.when((qi > 0) & (tail == t))
        def _(t=t):
            run(qi - t, t, False)

    lam = _lambda(lq1, lk1, lq2, lk2, lam_init)
    acc = acc_scr[...]
    _attn_finish(acc[0:tq, 0:LANES] / acc[0:tq, LANES:], acc[tq:, 0:LANES] / acc[tq:, LANES:],
                 lam, sg_ref, z_ref[...], o_ref, slice(None), lam_init)


def _attn_prompt(q3, k3, v3, z3, lq1, lk1, lq2, lk2, sg, lam_init, tq, tk):
    bsz, tlen, hd = q3.shape
    nh = hd // LANES
    assert tlen % tk == 0 and tk == tq and tq % CHUNK == 0
    qblk = pl.BlockSpec((None, tq, LANES), lambda bi, hi, qi: (bi, qi, hi))
    kblk = pl.BlockSpec((None, tlen, LANES), lambda bi, hi, qi: (bi, 0, hi))
    vec = lambda n: pl.BlockSpec((1, n), lambda bi, hi, qi: (0, 0))
    return pl.pallas_call(
        functools.partial(_attn_prompt_kernel, tq=tq, tk=tk, lam_init=lam_init),
        grid=(bsz, nh, tlen // tq),
        in_specs=[qblk, kblk, kblk, qblk, vec(HC_DIM), vec(HC_DIM), vec(HC_DIM), vec(HC_DIM), vec(LANES)],
        out_specs=qblk,
        out_shape=jax.ShapeDtypeStruct((bsz, tlen, hd), MIXER_OUT_DTYPE),
        scratch_shapes=[pltpu.VMEM((tlen, LANES), BF16), pltpu.VMEM((tlen, 2 * LANES), BF16),
                        pltpu.VMEM((2 * tq, LANES), F32), pltpu.VMEM((2 * tq, 2 * LANES), F32),
                        pltpu.VMEM((2 * tq, tk), F32), pltpu.VMEM((tq, tk), F32)],
        compiler_params=_params(("parallel", "parallel", "arbitrary")),
        name="attn_prompt",
    )(q3, k3, v3, z3, lq1, lk1, lq2, lk2, sg)


def _attn_sample_kernel(q_ref, ck_ref, cv_ref, kn_ref, vn_ref, z_ref, lq1, lk1, lq2, lk2, sg_ref, o_ref,
                        m_scr, l_scr, acc_scr, *, lam_init, past):
    kb = pl.program_id(2)
    tlen = q_ref.shape[0]
    tkc, heads = ck_ref.shape[0], ck_ref.shape[1]
    per_head = 2 * tlen
    ncol = heads * per_head

    @pl.when(kb == 0)
    def _():
        m_scr[...] = jnp.full(m_scr.shape, -jnp.inf, F32)
        l_scr[...] = jnp.zeros(l_scr.shape, F32)
        acc_scr[...] = jnp.zeros(acc_scr.shape, F32)

    head_cols = [slice(h * LANES, (h + 1) * LANES) for h in range(heads)]
    scale = (HC_DIM ** -0.5) * LOG2E
    qw = jnp.concatenate([_map_queries(q_ref[:, cs], scale) for cs in head_cols], axis=0)
    col = lax.broadcasted_iota(jnp.int32, (1, ncol), 1)
    col_head = lax.div(col, per_head)
    q_chunk = lax.div(past + lax.rem(col, tlen), CHUNK)

    def update(k_ref, v_ref, row_head, row_pos):
        k, v = k_ref().astype(BF16), v_ref().astype(BF16)
        n, cyc = k.shape[0], row_head.shape[0]
        own = jnp.where(row_head == col_head, 0.0, -jnp.inf)
        pieces = DECODE_ROW_PIECES if n % (DECODE_ROW_PIECES * cyc) == 0 else 1
        step = n // pieces
        scores = lambda i: _dot_nt(k[i * step:(i + 1) * step], qw)
        s_next = scores(0)
        for i in range(pieces):
            s = s_next
            if i + 1 < pieces:
                s_next = scores(i + 1)
            s = (s.reshape(step // cyc, cyc, ncol) + own[None]).reshape(step, ncol)
            if row_pos is not None:
                s = jnp.where(lax.div(row_pos[i * step:(i + 1) * step], CHUNK) <= q_chunk, s, -jnp.inf)
            m_old = m_scr[...]
            m_new = jnp.maximum(m_old, jnp.max(s, axis=0, keepdims=True))
            alpha = jnp.exp2(m_old - m_new)
            p = jnp.exp2(s - m_new)
            l_scr[...] = alpha * l_scr[...] + jnp.sum(p, axis=0, keepdims=True)
            acc_scr[...] = alpha * acc_scr[...] + _dot_tn(v[i * step:(i + 1) * step], p.astype(BF16))
            m_scr[...] = m_new

    cached_k = lambda: ck_ref[...].reshape(tkc * heads, LANES)
    cached_v = lambda: cv_ref[...].reshape(tkc * heads, LANES)
    cached_head = lax.broadcasted_iota(jnp.int32, (heads, 1), 0)
    all_visible = (kb * tkc + tkc - 1) // CHUNK <= past // CHUNK

    @pl.when(all_visible)
    def _():
        update(cached_k, cached_v, cached_head, None)

    @pl.when(jnp.logical_not(all_visible))
    def _():
        row = lax.broadcasted_iota(jnp.int32, (tkc * heads, 1), 0)
        update(cached_k, cached_v, cached_head, kb * tkc + lax.div(row, heads))

    @pl.when(kb == pl.num_programs(2) - 1)
    def _():
        row_n = lax.broadcasted_iota(jnp.int32, (tlen * heads, 1), 0)
        update(lambda: jnp.concatenate([kn_ref[:, cs] for cs in head_cols], axis=0),
               lambda: jnp.concatenate([vn_ref[:, cs] for cs in head_cols], axis=0),
               lax.div(row_n, tlen), past + lax.rem(row_n, tlen))
        o = (acc_scr[...] / l_scr[...]).T
        lam = _lambda(lq1, lk1, lq2, lk2, lam_init)
        for h, cs in enumerate(head_cols):
            o_h = o[h * per_head:(h + 1) * per_head]
            _attn_finish(o_h[0:tlen], o_h[tlen:], lam, sg_ref, z_ref[:, cs], o_ref, cs, lam_init)


def _attn_sample(q3, ck5, cv5, layer, kn3, vn3, z3, lq1, lk1, lq2, lk2, sg, lam_init, tkc, heads):
    bsz, tlen, hd = q3.shape
    past, nh = ck5.shape[2], ck5.shape[3]
    assert past % tkc == 0 and nh % heads == 0
    ncol = heads * 2 * tlen
    new = pl.BlockSpec((None, tlen, heads * LANES), lambda bi, hi, ki: (bi, 0, hi))
    old = pl.BlockSpec((None, None, tkc, heads, LANES), lambda bi, hi, ki: (layer, bi, ki, hi, 0))
    vec = lambda n: pl.BlockSpec((1, n), lambda bi, hi, ki: (0, 0))
    stat = pltpu.VMEM((1, ncol), F32)
    return pl.pallas_call(
        functools.partial(_attn_sample_kernel, lam_init=lam_init, past=past),
        grid=(bsz, nh // heads, past // tkc),
        in_specs=[new, old, old, new, new, new, vec(HC_DIM), vec(HC_DIM), vec(HC_DIM), vec(HC_DIM), vec(LANES)],
        out_specs=new,
        out_shape=jax.ShapeDtypeStruct((bsz, tlen, hd), MIXER_OUT_DTYPE),
        scratch_shapes=[stat, stat, pltpu.VMEM((LANES, ncol), F32)],
        compiler_params=_params(("parallel", "parallel", "arbitrary")),
        name="attn_sample",
    )(q3, ck5, cv5, kn3, vn3, z3, lq1, lk1, lq2, lk2, sg)


def _row_tile(n, pref):
    while n % pref:
        pref //= 2
    return pref


def _even_layer(x3, conv_buf, s0, w_in, w_out, cw, cb, cg, cbb, lb_logits, hn_g, pg, pb, layer, alpha):
    bsz, tlen, d = x3.shape
    n = bsz * tlen
    x2 = x3.reshape(n, d)
    d_a = cw.shape[-1]
    (h,) = _linear(x2, w_in, 1, _row_tile(n, LINEAR_ROWS), w_in.shape[1] // EVEN_PROJ_COL_TILES)
    h3 = h.reshape(bsz, tlen, -1)
    buf = jnp.pad(conv_buf, ((0, 0), (LEAD, 0), (0, 0)))
    out_a, nbuf = _conv_branch(h3, buf, cw, cb, cg, cbb, _row_tile(tlen, CONV_TIME))
    tc = _row_tile(tlen, HGRN_TIME) if tlen % CHUNK == 0 else tlen
    d_b = s0.shape[1] * s0.shape[2]
    out_b, s_new = _hgrn_branch(h3, lb_logits, hn_g, s0, layer, 3 * d_a // d_b, tc, HGRN_HEADS_PER_STEP)
    xn = _out_ln(out_a.reshape(n, -1), 0, out_b.reshape(n, -1), 0, w_out, x2, pg, pb, alpha,
                 _row_tile(n, OUT_LN_ROWS))
    return xn.reshape(bsz, tlen, d), nbuf[:, LEAD:], s_new


def _odd_layer(x3, caches, w_in, w_out, lq1, lk1, lq2, lk2, sg, pg, pb, lam_init, alpha):
    bsz, tlen, d = x3.shape
    n = bsz * tlen
    x2 = x3.reshape(n, d)
    d_c = w_in.shape[1] // 4
    q, k, v, z = [y.reshape(bsz, tlen, d_c)
                  for y in _linear(x2, w_in, 4, _row_tile(n, LINEAR_ROWS), _row_tile(d_c, LINEAR_COLS))]
    if caches is None:
        tk = _row_tile(tlen, ATTN_BLOCK)
        o = _attn_prompt(q, k, v, z, lq1, lk1, lq2, lk2, sg, lam_init, tk, tk)
    else:
        ck, cv, layer = caches
        o = _attn_sample(q, ck, cv, layer, k, v, z, lq1, lk1, lq2, lk2, sg, lam_init,
                         _row_tile(ck.shape[2], DECODE_KEYS), SUBLANES)
    o2 = o.reshape(n, d_c)
    xn = _out_ln(o2, 0, o2, 1, w_out, x2, pg, pb, alpha, _row_tile(n, OUT_LN_ROWS))
    nh = d_c // LANES
    return xn.reshape(bsz, tlen, d), k.reshape(bsz, tlen, nh, LANES), v.reshape(bsz, tlen, nh, LANES)


def kernel(x_prompt, x_sample, state_conv, state_hgrn, cache_k, cache_v, w_in_even, w_out_even, conv_w, conv_b,
           conv_ln_g, conv_ln_b, hgrn_lb_logits, hgrn_norm_g, w_in_odd, w_out_odd, lam_q1, lam_k1, lam_q2, lam_k2,
           subln_g, post_ln_g, post_ln_b):
    depth = post_ln_g.shape[0]
    alpha = (2.0 * depth) ** 0.25
    xp, xs = x_prompt, x_sample
    bp = xp.shape[0]
    row = lambda a: a.reshape(1, -1)
    outs = {name: [] for name in ("conv_p", "hgrn_p", "k_p", "v_p", "conv_s", "hgrn_s", "k_s", "v_s")}
    for l in range(depth):
        pg, pb = row(post_ln_g[l]), row(post_ln_b[l])
        if l % 2 == 0:
            e = l // 2
            prm = (w_in_even[e].astype(BF16), w_out_even[e].astype(BF16), conv_w[e], row(conv_b[e]),
                   row(conv_ln_g[e]), row(conv_ln_b[e]), hgrn_lb_logits, row(hgrn_norm_g[e]), pg, pb, e, alpha)
            buf0 = jnp.zeros((bp,) + state_conv.shape[2:], F32)
            s0 = jnp.zeros((bp,) + state_hgrn.shape[2:], F32)
            xp, cb_p, s_p = _even_layer(xp, buf0, s0, *prm)
            xs, cb_s, s_s = _even_layer(xs, state_conv[e], state_hgrn[e], *prm)
            outs["conv_p"].append(cb_p); outs["hgrn_p"].append(s_p)
            outs["conv_s"].append(cb_s); outs["hgrn_s"].append(s_s)
        else:
            o = l // 2
            lam_init = 0.8 - 0.6 * math.exp(-0.3 * l)
            prm = (w_in_odd[o].astype(BF16), w_out_odd[o].astype(BF16), row(lam_q1[o]), row(lam_k1[o]),
                   row(lam_q2[o]), row(lam_k2[o]), row(subln_g[o]), pg, pb, lam_init, alpha)
            xp, kn_p, vn_p = _odd_layer(xp, None, *prm)
            xs, kn_s, vn_s = _odd_layer(xs, (cache_k, cache_v, o), *prm)
            outs["k_p"].append(kn_p); outs["v_p"].append(vn_p)
            outs["k_s"].append(kn_s); outs["v_s"].append(vn_s)
    st = lambda name: jnp.stack(outs[name])
    return (xp, xs, st("conv_p"), st("hgrn_p"), st("k_p"), st("v_p"),
            st("conv_s"), st("hgrn_s"), st("k_s"), st("v_s"))
```
